```python
import math
import jax
import jax.numpy as jnp
from jax import lax
import numpy as np

D_MODEL = 4096
BATCH = 1
SEQ = 16384
DEPTH = 4

CTX_LEN = 256
GRID_W = 64
D_MIX = D_MODEL
HEAD_DIM = 128
ROPE_THETA = 10000.0
NORM_EPS = 1e-6

SSD_INNER = D_MIX // 2
SSD_HEAD_DIM = 64
SSD_HEADS = SSD_INNER // SSD_HEAD_DIM
SSD_GROUPS = 8
SSD_STATE = 128
SSD_CHUNK = 128
CONV_K = 5
SSD_XBC = SSD_INNER + 2 * SSD_GROUPS * SSD_STATE

SWA_WIDTH = D_MIX // 4
SWA_HEADS = SWA_WIDTH // HEAD_DIM
SWA_KV_HEADS = 2
WINDOW = 128
ATT_BLOCK = 128

GLB_WIDTH = D_MIX - SSD_INNER - SWA_WIDTH
GLB_HEADS = GLB_WIDTH // HEAD_DIM
GLB_KV_HEADS = 2

PEER_HEADS = 8
PEER_KEY_DIM = 256
PEER_HALF = PEER_KEY_DIM // 2
N_KEYS = 80
PEER_TOPK = 16
N_EXPERTS = N_KEYS * N_KEYS
PEER_TOKEN_BLOCK = 64

IN_SIZES = (SSD_INNER, SSD_XBC, 2 * SSD_HEADS,
            SWA_WIDTH, SWA_KV_HEADS * HEAD_DIM, SWA_KV_HEADS * HEAD_DIM,
            GLB_WIDTH, GLB_KV_HEADS * HEAD_DIM, GLB_KV_HEADS * HEAD_DIM)
IN_SPLITS = tuple(sum(IN_SIZES[:i + 1]) for i in range(len(IN_SIZES) - 1))
N_IN = sum(IN_SIZES)

kernel_name = "hybrid_ssd_swa_axial_peer_dit_block"


def rmsnorm(x, g):
    xf = x.astype(jnp.float32)
    y = xf * lax.rsqrt(jnp.mean(xf * xf, axis=-1, keepdims=True) + NORM_EPS)
    return (y * g.astype(jnp.float32)).astype(x.dtype)


def axial_rope_tables(n_tok, dtype):
    rows = n_tok // GRID_W
    pos_r = jnp.repeat(jnp.arange(rows, dtype=jnp.float32), GRID_W)
    pos_c = jnp.tile(jnp.arange(GRID_W, dtype=jnp.float32), rows)
    n_freq = HEAD_DIM // 4
    inv = ROPE_THETA ** (-jnp.arange(n_freq, dtype=jnp.float32) / n_freq)
    ang = jnp.stack([pos_r[:, None] * inv, pos_c[:, None] * inv], axis=1)
    return jnp.cos(ang).astype(dtype), jnp.sin(ang).astype(dtype)


def apply_axial_rope(x, cos, sin):
    b, n, h, d = x.shape
    xr = x.reshape(b, n, h, 2, 2, d // 4)
    x1, x2 = xr[..., 0, :], xr[..., 1, :]
    cs, sn = cos[None, :, None], sin[None, :, None]
    out = jnp.stack([x1 * cs - x2 * sn, x2 * cs + x1 * sn], axis=-2)
    return out.reshape(b, n, h, d)


def centred_dwconv(x, w, bias):
    y = lax.conv_general_dilated(x, w[:, None, :].astype(x.dtype), window_strides=(1,),
                                 padding=[(CONV_K // 2, CONV_K // 2)],
                                 dimension_numbers=('NWC', 'WIO', 'NWC'),
                                 feature_group_count=x.shape[-1])
    return y + bias.astype(x.dtype)


def ssd_scan(x, dt, a, bm, cm, h0):
    b, l, h, p = x.shape
    g, n = bm.shape[2], bm.shape[3]
    r = h // g
    q = SSD_CHUNK
    nc = l // q
    x = x.reshape(b, nc, q, g, r, p)
    dt = dt.reshape(b, nc, q, g, r)
    bm = bm.reshape(b, nc, q, g, n)
    cm = cm.reshape(b, nc, q, g, n)
    acs = jnp.moveaxis(jnp.cumsum(dt * a.reshape(g, r), axis=2), 2, -1)
    lower = jnp.tril(jnp.ones((q, q), dtype=bool))
    seg = jnp.exp(jnp.where(lower, acs[..., :, None] - acs[..., None, :], -jnp.inf))
    xdt = x * dt[..., None]
    cb = jnp.einsum('bcign,bcjgn->bcgij', cm, bm)
    y_diag = jnp.einsum('bcgrij,bcjgrp->bcigrp', cb[:, :, :, None] * seg, xdt)
    decay_out = jnp.exp(acs[..., -1:] - acs)
    states = jnp.einsum('bcjgn,bcgrj,bcjgrp->bcgrpn', bm, decay_out, xdt)
    chunk_decay = jnp.exp(acs[..., -1])

    def step(hc, inp):
        dec, st = inp
        return dec[..., None, None] * hc + st, hc

    h_fin, h_in = lax.scan(step, h0, (jnp.moveaxis(chunk_decay, 1, 0), jnp.moveaxis(states, 1, 0)))
    h_in = jnp.moveaxis(h_in, 0, 1)
    y_off = jnp.einsum('bcign,bcgrpn,bcgri->bcigrp', cm, h_in, jnp.exp(acs))
    return (y_diag + y_off).reshape(b, l, h, p), h_fin


def gated_rmsnorm(y, z, g):
    u = y * jax.nn.silu(z.astype(jnp.float32))
    shp = u.shape
    u = u.reshape(shp[:-1] + (SSD_GROUPS, shp[-1] // SSD_GROUPS))
    u = u * lax.rsqrt(jnp.mean(u * u, axis=-1, keepdims=True) + NORM_EPS)
    return (u.reshape(shp) * g.astype(jnp.float32)).astype(z.dtype)


def ssd_mixer(z_c, xbc_c, dt_c, z_l, xbc_l, dt_l, conv_w, conv_b, dt_bias, a_log, d_skip, norm_g, need_ctx):
    def prep(xbc, dt_raw):
        u = jax.nn.silu(centred_dwconv(xbc, conv_w, conv_b)).astype(jnp.float32)
        b_, l_ = u.shape[:2]
        xs, bm, cm = jnp.split(u, [SSD_INNER, SSD_INNER + SSD_GROUPS * SSD_STATE], axis=-1)
        xs = xs.reshape(b_, l_, SSD_HEADS, SSD_HEAD_DIM)
        bm = bm.reshape(b_, l_, SSD_GROUPS, SSD_STATE)
        cm = cm.reshape(b_, l_, SSD_GROUPS, SSD_STATE)
        dt = jax.nn.softplus(dt_raw.astype(jnp.float32).reshape(b_, l_, 2, SSD_HEADS)
                             + dt_bias.astype(jnp.float32))
        return xs, bm, cm, dt

    xc, bc, cc, dtc = prep(xbc_c, dt_c)
    xl, bl, cl, dtl = prep(xbc_l, dt_l)
    a = -jnp.exp(a_log.astype(jnp.float32))
    h0 = jnp.zeros((xc.shape[0], SSD_GROUPS, SSD_HEADS // SSD_GROUPS, SSD_HEAD_DIM, SSD_STATE), jnp.float32)
    flip = lambda t: jnp.flip(t, axis=1)
    yc_f, hc_f = ssd_scan(xc, dtc[:, :, 0], a[0], bc, cc, h0)
    yl_f, _ = ssd_scan(xl, dtl[:, :, 0], a[0], bl, cl, hc_f)
    yc_b, hc_b = ssd_scan(flip(xc), flip(dtc[:, :, 1]), a[1], flip(bc), flip(cc), h0)
    yl_b, _ = ssd_scan(flip(xl), flip(dtl[:, :, 1]), a[1], flip(bl), flip(cl), hc_b)
    dsk = d_skip.astype(jnp.float32)[:, None]

    def finish(yf, yb_rev, xs, z):
        y = yf + flip(yb_rev) + dsk * xs
        return gated_rmsnorm(y.reshape(y.shape[0], y.shape[1], SSD_INNER), z, norm_g)

    out_l = finish(yl_f, yl_b, xl, z_l)
    out_c = finish(yc_f, yc_b, xc, z_c) if need_ctx else None
    return out_c, out_l


def dense_attend(q, k, v, sink):
    b, nq, hk, gq, d = q.shape
    s = jnp.einsum('bqkgd,bskd->bkgqs', q, k).astype(jnp.float32) * d ** -0.5
    if sink is None:
        p = jax.nn.softmax(s, axis=-1)
    else:
        sink_b = jnp.broadcast_to(sink.astype(jnp.float32).reshape(1, hk, gq, 1, 1), s.shape[:-1] + (1,))
        p = jax.nn.softmax(jnp.concatenate([s, sink_b], axis=-1), axis=-1)[..., :-1]
    return jnp.einsum('bkgqs,bskd->bqkgd', p.astype(v.dtype), v)


def swa_latent(q, k, v, kc, vc, sink):
    b, s, hk, gq, d = q.shape
    nb = s // ATT_BLOCK
    nw = 3 * ATT_BLOCK
    nctx = kc.shape[1]
    qb = q.reshape(b, nb, ATT_BLOCK, hk, gq, d)

    def band(t):
        tp = jnp.pad(t, ((0, 0), (ATT_BLOCK, ATT_BLOCK), (0, 0), (0, 0))).reshape(b, nb + 2, ATT_BLOCK, hk, d)
        return jnp.concatenate([tp[:, :-2], tp[:, 1:-1], tp[:, 2:]], axis=2)

    kw, vw = band(k), band(v)
    scale = d ** -0.5
    s_win = jnp.einsum('bnqkgd,bnskd->bnkgqs', qb, kw).astype(jnp.float32) * scale
    s_ctx = jnp.einsum('bnqkgd,bckd->bnkgqc', qb, kc).astype(jnp.float32) * scale
    rel = jnp.arange(nw)[None, :] - ATT_BLOCK - jnp.arange(ATT_BLOCK)[:, None]
    kpos = (jnp.arange(nb)[:, None] - 1) * ATT_BLOCK + jnp.arange(nw)[None, :]
    mask = (jnp.abs(rel) <= WINDOW)[None] & ((kpos >= 0) & (kpos < s))[:, None, :]
    s_win = jnp.where(mask[None, :, None, None], s_win, -jnp.inf)
    sink_b = jnp.broadcast_to(sink.astype(jnp.float32).reshape(1, 1, hk, gq, 1, 1), s_win.shape[:-1] + (1,))
    p = jax.nn.softmax(jnp.concatenate([s_win, s_ctx, sink_b], axis=-1), axis=-1).astype(v.dtype)
    o = (jnp.einsum('bnkgqs,bnskd->bnqkgd', p[..., :nw], vw)
         + jnp.einsum('bnkgqc,bckd->bnqkgd', p[..., nw:nw + nctx], vc))
    return o.reshape(b, s, hk * gq * d)


def global_latent(q, k_all, v_all):
    b, s, hk, gq, d = q.shape
    nb = s // ATT_BLOCK
    qb = jnp.moveaxis(q.reshape(b, nb, ATT_BLOCK, hk, gq, d), 1, 0)
    o = lax.map(lambda qi: dense_attend(qi, k_all, v_all, None), qb)
    return jnp.moveaxis(o, 0, 1).reshape(b, s, hk * gq * d)


def token_mixers(pc, pl, cos, sin, conv_w, conv_b, dt_bias, a_log, d_skip, ssd_norm_g,
                 swa_sink, q_norm_g, k_norm_g, need_ctx):
    zc, xbcc, dtc, sqc, skc, svc, gqc, gkc, gvc = jnp.split(pc, IN_SPLITS, axis=-1)
    zl, xbcl, dtl, sql, skl, svl, gql, gkl, gvl = jnp.split(pl, IN_SPLITS, axis=-1)
    b, nctx = pc.shape[0], pc.shape[1]

    def heads(t):
        return t.reshape(t.shape[0], t.shape[1], t.shape[2] // HEAD_DIM, HEAD_DIM)

    def groups(t, hk):
        return t.reshape(t.shape[0], t.shape[1], hk, t.shape[2] // hk, HEAD_DIM)

    ssd_c, ssd_l = ssd_mixer(zc, xbcc, dtc, zl, xbcl, dtl, conv_w, conv_b, dt_bias, a_log,
                             d_skip, ssd_norm_g, need_ctx)
    swa_q = groups(apply_axial_rope(heads(sql), cos, sin), SWA_KV_HEADS)
    swa_k = apply_axial_rope(heads(skl), cos, sin)
    swa_kc, swa_vc = heads(skc), heads(svc)
    swa_l = swa_latent(swa_q, swa_k, heads(svl), swa_kc, swa_vc, swa_sink)
    glb_q = groups(apply_axial_rope(rmsnorm(heads(gql), q_norm_g), cos, sin), GLB_KV_HEADS)
    glb_k = apply_axial_rope(rmsnorm(heads(gkl), k_norm_g), cos, sin)
    glb_kc, glb_vc = rmsnorm(heads(gkc), k_norm_g), heads(gvc)
    k_all = jnp.concatenate([glb_kc, glb_k], axis=1)
    v_all = jnp.concatenate([glb_vc, heads(gvl)], axis=1)
    glb_l = global_latent(glb_q, k_all, v_all)
    out_l = jnp.concatenate([ssd_l, swa_l, glb_l], axis=-1)
    if not need_ctx:
        return None, out_l
    swa_c = dense_attend(groups(heads(sqc), SWA_KV_HEADS), swa_kc, swa_vc, swa_sink).reshape(b, nctx, SWA_WIDTH)
    glb_c = dense_attend(groups(rmsnorm(heads(gqc), q_norm_g), GLB_KV_HEADS), glb_kc, glb_vc, None).reshape(b, nctx, GLB_WIDTH)
    out_c = jnp.concatenate([ssd_c, swa_c, glb_c], axis=-1)
    return out_c, out_l


def peer_ffn(h, wq, sub_keys, exp_u, exp_v):
    b, t, d = h.shape
    hb = h.reshape(-1, PEER_TOKEN_BLOCK, d)

    def blk(xb):
        q = (xb @ wq).reshape(PEER_TOKEN_BLOCK, PEER_HEADS, 2, PEER_HALF)
        s = jnp.einsum('thsd,hskd->thsk', q, sub_keys).astype(jnp.float32)
        sv, si = lax.top_k(s, PEER_TOPK)
        cand = (sv[:, :, 0, :, None] + sv[:, :, 1, None, :]).reshape(PEER_TOKEN_BLOCK, PEER_HEADS, PEER_TOPK * PEER_TOPK)
        sc, ci = lax.top_k(cand, PEER_TOPK)
        i1 = jnp.take_along_axis(si[:, :, 0], ci // PEER_TOPK, axis=-1)
        i2 = jnp.take_along_axis(si[:, :, 1], ci % PEER_TOPK, axis=-1)
        e = i1 * N_KEYS + i2
        g = jax.nn.softmax(sc, axis=-1)
        act = jax.nn.gelu(jnp.einsum('thkd,td->thk', exp_u[e], xb).astype(jnp.float32), approximate=False)
        return jnp.einsum('thk,thkd->td', (g * act).astype(xb.dtype), exp_v[e])

    return lax.map(blk, hb).reshape(b, t, d)


def setup_inputs(seed: int = 0) -> dict:
    key = jax.random.key(seed)
    ks = jax.random.split(key, 24)
    f32 = jnp.float32

    def nrm(k, shape, scale):
        return jax.random.normal(k, shape, f32) * scale

    def gain(k, shape):
        return 1.0 + 0.1 * jax.random.normal(k, shape, f32)

    dt0 = jnp.exp(jax.random.uniform(ks[10], (DEPTH, 2, SSD_HEADS), f32, math.log(1e-3), math.log(1e-1)))
    return {
        "x": nrm(ks[0], (BATCH, SEQ, D_MODEL), 1.0),
        "c": nrm(ks[1], (BATCH, D_MODEL), 1.0),
        "ctx": nrm(ks[2], (BATCH, CTX_LEN, D_MODEL), 1.0),
        "c_ctx": nrm(ks[3], (D_MODEL,), 1.0),
        "ada_w": nrm(ks[4], (DEPTH, D_MODEL, 6 * D_MODEL), D_MODEL ** -0.5),
        "ada_b": nrm(ks[5], (DEPTH, 6 * D_MODEL), 0.01),
        "norm1_g": gain(ks[6], (DEPTH, D_MODEL)),
        "w_in": nrm(ks[7], (DEPTH, D_MODEL, N_IN), D_MODEL ** -0.5),
        "conv_w": nrm(ks[8], (DEPTH, CONV_K, SSD_XBC), CONV_K ** -0.5),
        "conv_b": nrm(ks[9], (DEPTH, SSD_XBC), 0.01),
        "dt_bias": dt0 + jnp.log(-jnp.expm1(-dt0)),
        "a_log": jnp.log(jax.random.uniform(ks[11], (DEPTH, 2, SSD_HEADS), f32, 1.0, 16.0)),
        "d_skip": gain(ks[12], (DEPTH, SSD_HEADS)),
        "ssd_norm_g": gain(ks[13], (DEPTH, SSD_INNER)),
        "swa_sink": nrm(ks[14], (DEPTH, SWA_HEADS), 0.5),
        "q_norm_g": gain(ks[15], (DEPTH, HEAD_DIM)),
        "k_norm_g": gain(ks[16], (DEPTH, HEAD_DIM)),
        "w_out": nrm(ks[17], (DEPTH, D_MIX, D_MODEL), D_MIX ** -0.5),
        "norm2_g": gain(ks[18], (DEPTH, D_MODEL)),
        "peer_wq": nrm(ks[19], (DEPTH, D_MODEL, PEER_HEADS * PEER_KEY_DIM), D_MODEL ** -0.5),
        "peer_keys": nrm(ks[20], (DEPTH, PEER_HEADS, 2, N_KEYS, PEER_HALF), PEER_HALF ** -0.5),
        "peer_u": nrm(ks[21], (DEPTH, N_EXPERTS, D_MODEL), D_MODEL ** -0.5),
        "peer_v": nrm(ks[22], (DEPTH, N_EXPERTS, D_MODEL), PEER_HEADS ** -0.5),
        "final_g": gain(ks[23], (D_MODEL,)),
    }


def reference(x, c, ctx, c_ctx, ada_w, ada_b, norm1_g, w_in, conv_w, conv_b, dt_bias, a_log,
              d_skip, ssd_norm_g, swa_sink, q_norm_g, k_norm_g, w_out, norm2_g, peer_wq,
              peer_keys, peer_u, peer_v, final_g):
    b, s, d = x.shape
    cos, sin = axial_rope_tables(s, x.dtype)
    silu_c = jax.nn.silu(c)
    silu_cc = jax.nn.silu(c_ctx)
    lat, cx = x, ctx
    for l in range(DEPTH):
        need_ctx = l < DEPTH - 1
        ml = jnp.split((silu_c @ ada_w[l] + ada_b[l])[:, None, :], 6, axis=-1)
        mc = jnp.split(silu_cc @ ada_w[l] + ada_b[l], 6, axis=-1)
        hl = rmsnorm(lat, norm1_g[l]) * (1 + ml[1]) + ml[0]
        hc = rmsnorm(cx, norm1_g[l]) * (1 + mc[1]) + mc[0]
        out_c, out_l = token_mixers(hc @ w_in[l], hl @ w_in[l], cos, sin, conv_w[l], conv_b[l],
                                    dt_bias[l], a_log[l], d_skip[l], ssd_norm_g[l], swa_sink[l],
                                    q_norm_g[l], k_norm_g[l], need_ctx)
        lat = lat + ml[2] * (out_l @ w_out[l])
        hl = rmsnorm(lat, norm2_g[l]) * (1 + ml[4]) + ml[3]
        if need_ctx:
            cx = cx + mc[2] * (out_c @ w_out[l])
            hc = rmsnorm(cx, norm2_g[l]) * (1 + mc[4]) + mc[3]
            f = peer_ffn(jnp.concatenate([hc, hl], axis=1), peer_wq[l], peer_keys[l], peer_u[l], peer_v[l])
            nctx = cx.shape[1]
            cx = cx + mc[5] * f[:, :nctx]
            lat = lat + ml[5] * f[:, nctx:]
        else:
            lat = lat + ml[5] * peer_ffn(hl, peer_wq[l], peer_keys[l], peer_u[l], peer_v[l])
    return rmsnorm(lat, final_g)
```

```python
import functools
import math

import jax
import jax.numpy as jnp
import numpy as np
from jax import lax
from jax.experimental import pallas as pl
from jax.experimental.pallas import tpu as pltpu

F32 = jnp.float32
MXU_DTYPE = jnp.bfloat16

HEAD_DIM = 128
GRID_W = 64
ROPE_THETA = 10000.0
NORM_EPS = 1e-6

SSD_HEAD_DIM = 64
SSD_GROUPS = 8
SSD_STATE = 128
SSD_CHUNK = 128
CONV_K = 5
CONV_HALO = 8

SWA_KV_HEADS = 2
GLB_KV_HEADS = 2
ATT_BLOCK = 128

PEER_HEADS = 8
PEER_HALF = 128
PEER_TOPK = 16

NEG_BIG = -1e30
VMEM_LIMIT = 56 * 1024 * 1024


def _pick(n, candidates):
    for c in candidates:
        if n % c == 0:
            return c
    raise ValueError(f"no tile in {candidates} divides {n}")


def _params(*sem):
    return pltpu.CompilerParams(dimension_semantics=sem, vmem_limit_bytes=VMEM_LIMIT)


def _row_select(ref, row0, tm, nctx):
    rows = row0 + lax.broadcasted_iota(jnp.int32, (tm, 1), 0)
    return jnp.where(rows < nctx, ref[0:1, :], ref[1:2, :])


def _mods_kernel(c_ref, w_ref, b_ref, o_ref):
    cv = c_ref[...]
    sv = cv * (1.0 / (1.0 + jnp.exp(-cv)))
    acc = jnp.dot(sv.astype(MXU_DTYPE), w_ref[...].astype(MXU_DTYPE),
                  preferred_element_type=F32)
    o_ref[...] = acc + b_ref[...]


def _mods(cvec, ada_w, ada_b):
    depth, d, n = ada_w.shape
    tn = _pick(n, (512, 256, 128))
    return pl.pallas_call(
        _mods_kernel,
        grid=(depth, n // tn),
        in_specs=[
            pl.BlockSpec((8, d), lambda l, j: (0, 0)),
            pl.BlockSpec((None, d, tn), lambda l, j: (l, 0, j)),
            pl.BlockSpec((None, 1, tn), lambda l, j: (l, 0, j)),
        ],
        out_specs=pl.BlockSpec((None, 8, tn), lambda l, j: (l, 0, j)),
        out_shape=jax.ShapeDtypeStruct((depth, 8, n), F32),
        compiler_params=_params("parallel", "parallel"),
        name="adaln_mods",
    )(cvec, ada_w, ada_b.reshape(depth, 1, n))


def _normmod_kernel(x_ref, g_ref, shift_ref, scale_ref, o_ref, *, tm, nctx):
    x = x_ref[...]
    y = x * lax.rsqrt(jnp.mean(x * x, axis=-1, keepdims=True) + NORM_EPS)
    y = y * g_ref[...]
    row0 = pl.program_id(0) * tm
    scale = _row_select(scale_ref, row0, tm, nctx)
    shift = _row_select(shift_ref, row0, tm, nctx)
    o_ref[...] = (y * (1.0 + scale) + shift).astype(o_ref.dtype)


def _normmod(xs, g, shift, scale, nctx):
    m, d = xs.shape
    tm = _pick(m, (256, 128))
    return pl.pallas_call(
        functools.partial(_normmod_kernel, tm=tm, nctx=nctx),
        grid=(m // tm,),
        in_specs=[
            pl.BlockSpec((tm, d), lambda i: (i, 0)),
            pl.BlockSpec((1, d), lambda i: (0, 0)),
            pl.BlockSpec((2, d), lambda i: (0, 0)),
            pl.BlockSpec((2, d), lambda i: (0, 0)),
        ],
        out_specs=pl.BlockSpec((tm, d), lambda i: (i, 0)),
        out_shape=jax.ShapeDtypeStruct((m, d), MXU_DTYPE),
        compiler_params=_params("parallel"),
        name="norm_modulate",
    )(xs, g.reshape(1, d), shift, scale)


def _finalnorm_kernel(x_ref, g_ref, o_ref):
    x = x_ref[...]
    y = x * lax.rsqrt(jnp.mean(x * x, axis=-1, keepdims=True) + NORM_EPS)
    o_ref[...] = y * g_ref[...]


def _finalnorm(xs, g, nctx):
    m, d = xs.shape
    s = m - nctx
    tm = _pick(math.gcd(s, nctx), (256, 128))
    off = nctx // tm
    return pl.pallas_call(
        _finalnorm_kernel,
        grid=(s // tm,),
        in_specs=[
            pl.BlockSpec((tm, d), lambda i: (i + off, 0)),
            pl.BlockSpec((1, d), lambda i: (0, 0)),
        ],
        out_specs=pl.BlockSpec((tm, d), lambda i: (i, 0)),
        out_shape=jax.ShapeDtypeStruct((s, d), F32),
        compiler_params=_params("parallel"),
        name="final_norm",
    )(xs, g.reshape(1, d))


def _mm_kernel(a_ref, b_ref, o_ref):
    o_ref[...] = jnp.dot(a_ref[...], b_ref[...],
                         preferred_element_type=F32).astype(o_ref.dtype)


def _mm_res_kernel(a_ref, b_ref, res_ref, gate_ref, o_ref, *, tm, nctx):
    acc = jnp.dot(a_ref[...], b_ref[...], preferred_element_type=F32)
    gate = _row_select(gate_ref, pl.program_id(1) * tm, tm, nctx)
    o_ref[...] = res_ref[...] + gate * acc


def _matmul(a, b, out_dtype, *, res=None, gate=None, nctx=0, name="matmul"):
    m, k = a.shape
    n = b.shape[1]
    tm = _pick(m, (640, 256, 128))
    tn = _pick(n, (512, 640, 256, 128))
    a_spec = pl.BlockSpec((tm, k), lambda j, i: (i, 0))
    b_spec = pl.BlockSpec((k, tn), lambda j, i: (0, j))
    o_spec = pl.BlockSpec((tm, tn), lambda j, i: (i, j))
    if res is None:
        kern, specs, args = _mm_kernel, [a_spec, b_spec], (a, b)
    else:
        kern = functools.partial(_mm_res_kernel, tm=tm, nctx=nctx)
        specs = [a_spec, b_spec, o_spec, pl.BlockSpec((2, tn), lambda j, i: (0, j))]
        args = (a, b, res, gate)
    return pl.pallas_call(
        kern,
        grid=(n // tn, m // tm),
        in_specs=specs,
        out_specs=o_spec,
        out_shape=jax.ShapeDtypeStruct((m, n), out_dtype),
        compiler_params=_params("parallel", "parallel"),
        name=name,
    )(*args)


def _ssd_chunk_of_step(s, ncc, nch, reverse):
    if not reverse:
        return s
    return jnp.where(s < ncc, ncc - 1 - s, nch - 1 + ncc - s)


def _ssd_kernel(*refs, reverse, ncc, nch):
    if reverse:
        (xbc_ref, prev_ref, next_ref, dt_ref, cw_ref, cb_ref, dtb_ref, a_ref,
         z_ref, yf_ref, dsk_ref, ng_ref, o_ref, h_ref, xw_ref) = refs
    else:
        (xbc_ref, prev_ref, next_ref, dt_ref, cw_ref, cb_ref, dtb_ref, a_ref,
         o_ref, h_ref, xw_ref) = refs
    q = SSD_CHUNK
    step = pl.program_id(0)
    c = _ssd_chunk_of_step(step, ncc, nch, reverse)
    n_xs = SSD_GROUPS * 4 * SSD_HEAD_DIM
    n_bc = SSD_GROUPS * SSD_STATE
    n_heads = n_xs // SSD_HEAD_DIM

    @pl.when(step == 0)
    def _():
        h_ref[...] = jnp.zeros_like(h_ref)

    has_prev = jnp.logical_and(c != 0, c != ncc)
    has_next = jnp.logical_and(c != ncc - 1, c != nch - 1)
    xw_ref[0:CONV_HALO, :] = jnp.where(has_prev, prev_ref[...], 0.0)
    xw_ref[CONV_HALO:CONV_HALO + q, :] = xbc_ref[...]
    xw_ref[CONV_HALO + q:, :] = jnp.where(has_next, next_ref[...], 0.0)
    u = jnp.zeros((q, n_xs + 2 * n_bc), F32) + cb_ref[...]
    for kk in range(CONV_K):
        start = CONV_HALO - CONV_K // 2 + kk
        u = u + xw_ref[start:start + q, :] * cw_ref[kk:kk + 1, :]
    u = u * (1.0 / (1.0 + jnp.exp(-u)))

    dtr = dt_ref[...] + dtb_ref[...]
    dt = jnp.maximum(dtr, 0.0) + jnp.log1p(jnp.exp(-jnp.abs(dtr)))
    d_a = dt * a_ref[...]
    ii = lax.broadcasted_iota(jnp.int32, (q, q), 0)
    jj = lax.broadcasted_iota(jnp.int32, (q, q), 1)
    mask = (jj >= ii) if reverse else (jj <= ii)
    tri = jnp.where(mask, 1.0, 0.0).astype(F32)
    cs = jnp.dot(tri, d_a, preferred_element_type=F32, precision=lax.Precision.HIGHEST)
    cs_t = cs.T
    dt_t = dt.T
    last = 0 if reverse else q - 1
    total = cs[last:last + 1, :]
    w_state_t = (dt * jnp.exp(total - cs)).T
    e_cs = jnp.exp(cs)
    chunk_decay = jnp.exp(total)
    lane = lax.broadcasted_iota(jnp.int32, (1, 2 * SSD_HEAD_DIM), 1)
    first_half = lane < SSD_HEAD_DIM
    dir_off = n_heads if reverse else 0

    y_parts = []
    for g in range(SSD_GROUPS):
        bg = u[:, n_xs + g * SSD_STATE:n_xs + (g + 1) * SSD_STATE]
        cg = u[:, n_xs + n_bc + g * SSD_STATE:n_xs + n_bc + (g + 1) * SSD_STATE]
        cb = lax.dot_general(cg.astype(MXU_DTYPE), bg.astype(MXU_DTYPE),
                             (((1,), (1,)), ((), ())), preferred_element_type=F32)
        bg_t = bg.T
        for pair in range(2):
            lo = g * 4 * SSD_HEAD_DIM + pair * 2 * SSD_HEAD_DIM
            xs_pair = u[:, lo:lo + 2 * SSD_HEAD_DIM].astype(MXU_DTYPE)
            h_pair = h_ref[g, :, pair * 128:(pair + 1) * 128]
            rhs = jnp.concatenate([xs_pair, h_pair.astype(MXU_DTYPE)], axis=0)
            ys, sts, decs = [], [], []
            for r in range(2):
                col = dir_off + g * 4 + pair * 2 + r
                seg = jnp.exp(jnp.where(mask, cs[:, col:col + 1] - cs_t[col:col + 1, :], -jnp.inf))
                m_h = cb * seg * dt_t[col:col + 1, :]
                c_h = cg * e_cs[:, col:col + 1]
                lhs = jnp.concatenate([m_h.astype(MXU_DTYPE), c_h.astype(MXU_DTYPE)], axis=1)
                ys.append(jnp.dot(lhs, rhs, preferred_element_type=F32))
                b_h = bg_t * w_state_t[col:col + 1, :]
                sts.append(jnp.dot(b_h.astype(MXU_DTYPE), xs_pair, preferred_element_type=F32))
                decs.append(chunk_decay[:, col:col + 1])
            y_parts.append(jnp.where(first_half, ys[0], ys[1]))
            dec = jnp.where(first_half, decs[0], decs[1])
            h_ref[g, :, pair * 128:(pair + 1) * 128] = (
                h_pair * dec + jnp.where(first_half, sts[0], sts[1]))
    y = jnp.concatenate(y_parts, axis=1)

    if not reverse:
        o_ref[...] = y
    else:
        y = y + yf_ref[...] + dsk_ref[...] * u[:, :n_xs]
        zz = z_ref[...]
        y = y * (zz * (1.0 / (1.0 + jnp.exp(-zz))))
        gw = n_xs // SSD_GROUPS
        outs = []
        for g in range(SSD_GROUPS):
            sl = y[:, g * gw:(g + 1) * gw]
            outs.append(sl * lax.rsqrt(jnp.mean(sl * sl, axis=-1, keepdims=True) + NORM_EPS))
        o_ref[...] = (jnp.concatenate(outs, axis=1) * ng_ref[...]).astype(o_ref.dtype)


def _ssd(zx, p2, conv_w8, conv_b, dtb, a_vec, dsk, ng, nctx, *, reverse, yf=None):
    m = zx.shape[0]
    q = SSD_CHUNK
    nch, ncc = m // q, nctx // q
    n_xs = SSD_GROUPS * 4 * SSD_HEAD_DIM
    n_conv = n_xs + 2 * SSD_GROUPS * SSD_STATE
    dt_blk = (p2.shape[1] - 128) // 128
    hb = q // CONV_HALO
    n_hb = m // CONV_HALO

    def cidx(s):
        return _ssd_chunk_of_step(s, ncc, nch, reverse)

    in_specs = [
        pl.BlockSpec((q, n_conv), lambda s: (cidx(s), 0)),
        pl.BlockSpec((CONV_HALO, n_conv), lambda s: (jnp.maximum(cidx(s) * hb - 1, 0), 0)),
        pl.BlockSpec((CONV_HALO, n_conv), lambda s: (jnp.minimum((cidx(s) + 1) * hb, n_hb - 1), 0)),
        pl.BlockSpec((q, 128), lambda s: (cidx(s), dt_blk)),
        pl.BlockSpec((8, n_conv), lambda s: (0, 0)),
        pl.BlockSpec((1, n_conv), lambda s: (0, 0)),
        pl.BlockSpec((1, 128), lambda s: (0, 0)),
        pl.BlockSpec((1, 128), lambda s: (0, 0)),
    ]
    args = [zx, zx, zx, p2, conv_w8, conv_b, dtb, a_vec]
    if reverse:
        in_specs += [
            pl.BlockSpec((q, n_xs), lambda s: (cidx(s), n_conv // n_xs)),
            pl.BlockSpec((q, n_xs), lambda s: (cidx(s), 0)),
            pl.BlockSpec((1, n_xs), lambda s: (0, 0)),
            pl.BlockSpec((1, n_xs), lambda s: (0, 0)),
        ]
        args += [zx, yf, dsk, ng]
    return pl.pallas_call(
        functools.partial(_ssd_kernel, reverse=reverse, ncc=ncc, nch=nch),
        grid=(nch,),
        in_specs=in_specs,
        out_specs=pl.BlockSpec((q, n_xs), lambda s: (cidx(s), 0)),
        out_shape=jax.ShapeDtypeStruct((m, n_xs), MXU_DTYPE if reverse else F32),
        scratch_shapes=[
            pltpu.VMEM((SSD_GROUPS, SSD_STATE, 4 * SSD_HEAD_DIM), F32),
            pltpu.VMEM((q + 2 * CONV_HALO, n_conv), F32),
        ],
        compiler_params=_params("arbitrary"),
        name="ssd_bwd_finish" if reverse else "ssd_fwd",
    )(*args)


_QKV_SLOTS = (
    (8, None, True, True),
    (2, None, True, False),
    (2, None, False, False),
    (8, "q", True, True),
    (2, "k", True, False),
    (2, None, False, False),
)
N_QKV = sum(s[0] for s in _QKV_SLOTS) * HEAD_DIM


def _qkv_kernel(p_ref, cos_ref, sin_ref, qg_ref, kg_ref, o_ref):
    cos = cos_ref[...]
    sin = sin_ref[...]
    lane = lax.broadcasted_iota(jnp.int32, (1, HEAD_DIM), 1)
    lower = (lane % (HEAD_DIM // 2)) < (HEAD_DIM // 4)
    slot = 0
    for count, norm, rope, scale in _QKV_SLOTS:
        for _ in range(count):
            t = p_ref[:, slot * HEAD_DIM:(slot + 1) * HEAD_DIM]
            if norm is not None:
                g = qg_ref[...] if norm == "q" else kg_ref[...]
                t = t * lax.rsqrt(jnp.mean(t * t, axis=-1, keepdims=True) + NORM_EPS) * g
            if rope:
                swapped = jnp.where(lower, pltpu.roll(t, HEAD_DIM - HEAD_DIM // 4, 1),
                                    pltpu.roll(t, HEAD_DIM // 4, 1))
                t = t * cos + swapped * sin
            if scale:
                t = t * (HEAD_DIM ** -0.5)
            o_ref[:, slot * HEAD_DIM:(slot + 1) * HEAD_DIM] = t.astype(o_ref.dtype)
            slot += 1


def _qkv_prep(p2, cos, sin, qg, kg):
    m = p2.shape[0]
    tm = _pick(m, (256, 128))
    return pl.pallas_call(
        _qkv_kernel,
        grid=(m // tm,),
        in_specs=[
            pl.BlockSpec((tm, N_QKV), lambda i: (i, 0)),
            pl.BlockSpec((tm, HEAD_DIM), lambda i: (i, 0)),
            pl.BlockSpec((tm, HEAD_DIM), lambda i: (i, 0)),
            pl.BlockSpec((1, HEAD_DIM), lambda i: (0, 0)),
            pl.BlockSpec((1, HEAD_DIM), lambda i: (0, 0)),
        ],
        out_specs=pl.BlockSpec((tm, N_QKV), lambda i: (i, 0)),
        out_shape=jax.ShapeDtypeStruct((m, N_QKV), MXU_DTYPE),
        compiler_params=_params("parallel"),
        name="qkv_prep",
    )(p2, cos, sin, qg.reshape(1, HEAD_DIM), kg.reshape(1, HEAD_DIM))


def _rope_tables(s, nctx):
    t = jnp.arange(s, dtype=jnp.int32)
    pos_r = (t // GRID_W).astype(F32)
    pos_c = (t % GRID_W).astype(F32)
    n_freq = HEAD_DIM // 4
    inv = ROPE_THETA ** (-jnp.arange(n_freq, dtype=F32) / n_freq)
    ar, ac = pos_r[:, None] * inv, pos_c[:, None] * inv
    cos = jnp.concatenate([jnp.cos(ar), jnp.cos(ar), jnp.cos(ac), jnp.cos(ac)], axis=1)
    sin = jnp.concatenate([-jnp.sin(ar), jnp.sin(ar), -jnp.sin(ac), jnp.sin(ac)], axis=1)
    cos = jnp.concatenate([jnp.ones((nctx, HEAD_DIM), F32), cos], axis=0)
    sin = jnp.concatenate([jnp.zeros((nctx, HEAD_DIM), F32), sin], axis=0)
    return cos, sin


def _stack_heads(q, n):
    return jnp.concatenate([q[:, g * HEAD_DIM:(g + 1) * HEAD_DIM] for g in range(n)], axis=0)


def _unstack_heads(o, n, rows):
    return jnp.concatenate([o[g * rows:(g + 1) * rows, :] for g in range(n)], axis=1)


_NT = (((1,), (1,)), ((), ()))


def _glb_kernel(q_ref, k_ref, v_ref, o_ref, *, tq, tk, nk, gq):
    qs = _stack_heads(q_ref[...], gq)
    rows = gq * tq

    def body(j, carry):
        m_i, l_i, acc = carry
        start = pl.multiple_of(j * tk, tk)
        kb = k_ref[pl.ds(start, tk), :]
        vb = v_ref[pl.ds(start, tk), :]
        s = lax.dot_general(qs, kb, _NT, preferred_element_type=F32)
        m_new = jnp.maximum(m_i, jnp.max(s, axis=-1, keepdims=True))
        alpha = jnp.exp(m_i - m_new)
        p = jnp.exp(s - m_new)
        l_new = alpha * l_i + jnp.sum(p, axis=-1, keepdims=True)
        acc = alpha * acc + jnp.dot(p.astype(MXU_DTYPE), vb, preferred_element_type=F32)
        return m_new, l_new, acc

    init = (jnp.full((rows, 1), NEG_BIG, F32), jnp.zeros((rows, 1), F32),
            jnp.zeros((rows, HEAD_DIM), F32))
    _, l_i, acc = lax.fori_loop(0, nk, body, init)
    o_ref[...] = _unstack_heads(acc / l_i, gq, tq).astype(o_ref.dtype)


def _glb_attention(qkv, nctx):
    m = qkv.shape[0]
    s = m - nctx
    gq = 8 // GLB_KV_HEADS
    tq = _pick(math.gcd(s, nctx), (256, 128))
    tk = _pick(m, (640, 256, 128))
    qw = gq * HEAD_DIM
    q_blk0 = (12 * HEAD_DIM) // qw
    return pl.pallas_call(
        functools.partial(_glb_kernel, tq=tq, tk=tk, nk=m // tk, gq=gq),
        grid=(GLB_KV_HEADS, s // tq),
        in_specs=[
            pl.BlockSpec((tq, qw), lambda h, i: (i + nctx // tq, q_blk0 + h)),
            pl.BlockSpec((m, HEAD_DIM), lambda h, i: (0, 20 + h)),
            pl.BlockSpec((m, HEAD_DIM), lambda h, i: (0, 22 + h)),
        ],
        out_specs=pl.BlockSpec((tq, qw), lambda h, i: (i, h)),
        out_shape=jax.ShapeDtypeStruct((s, GLB_KV_HEADS * qw), MXU_DTYPE),
        compiler_params=_params("parallel", "parallel"),
        name="global_attention",
    )(qkv, qkv, qkv)


def _swa_kernel(q_ref, k_ref, v_ref, sink_ref, o_ref, *, nctx, nb, gq):
    blk = ATT_BLOCK
    n = pl.program_id(1)
    qs = _stack_heads(q_ref[...], gq)
    rows = gq * blk
    base = nctx + n * blk
    p_start = pl.multiple_of(base - blk, blk)
    c_start = pl.multiple_of(base, blk)
    n_start = pl.multiple_of(jnp.minimum(base + blk, nctx + (nb - 1) * blk), blk)
    ri = lax.broadcasted_iota(jnp.int32, (rows, blk), 0) % blk
    cj = lax.broadcasted_iota(jnp.int32, (rows, blk), 1)

    def scores(start, size):
        return lax.dot_general(qs, k_ref[pl.ds(start, size), :], _NT, preferred_element_type=F32)

    s_prev = jnp.where(jnp.logical_and(cj >= ri, n > 0), scores(p_start, blk), NEG_BIG)
    s_cur = scores(c_start, blk)
    s_next = jnp.where(jnp.logical_and(cj <= ri, n < nb - 1), scores(n_start, blk), NEG_BIG)
    s_ctx = scores(0, nctx)
    sink = sink_ref[:, 0:1]
    m_i = sink
    for s in (s_prev, s_cur, s_next, s_ctx):
        m_i = jnp.maximum(m_i, jnp.max(s, axis=-1, keepdims=True))
    l_i = jnp.exp(sink - m_i)
    acc = jnp.zeros((rows, HEAD_DIM), F32)
    for s, start, size in ((s_prev, p_start, blk), (s_cur, c_start, blk),
                           (s_next, n_start, blk), (s_ctx, 0, nctx)):
        p = jnp.exp(s - m_i)
        l_i = l_i + jnp.sum(p, axis=-1, keepdims=True)
        acc = acc + jnp.dot(p.astype(MXU_DTYPE), v_ref[pl.ds(start, size), :],
                            preferred_element_type=F32)
    o_ref[...] = _unstack_heads(acc / l_i, gq, blk).astype(o_ref.dtype)


def _sink_rows(sink, kv_heads, rows_per_head):
    gq = sink.shape[0] // kv_heads
    t = jnp.repeat(sink.reshape(kv_heads, gq).astype(F32), rows_per_head, axis=1)
    return jnp.broadcast_to(t[:, :, None], (kv_heads, gq * rows_per_head, 128))


def _swa_attention(qkv, sink, nctx):
    m = qkv.shape[0]
    s = m - nctx
    blk = ATT_BLOCK
    gq = 8 // SWA_KV_HEADS
    nb = s // blk
    qw = gq * HEAD_DIM
    return pl.pallas_call(
        functools.partial(_swa_kernel, nctx=nctx, nb=nb, gq=gq),
        grid=(SWA_KV_HEADS, nb),
        in_specs=[
            pl.BlockSpec((blk, qw), lambda h, i: (i + nctx // blk, h)),
            pl.BlockSpec((m, HEAD_DIM), lambda h, i: (0, 8 + h)),
            pl.BlockSpec((m, HEAD_DIM), lambda h, i: (0, 10 + h)),
            pl.BlockSpec((None, gq * blk, 128), lambda h, i: (h, 0, 0)),
        ],
        out_specs=pl.BlockSpec((blk, qw), lambda h, i: (i, h)),
        out_shape=jax.ShapeDtypeStruct((s, SWA_KV_HEADS * qw), MXU_DTYPE),
        compiler_params=_params("parallel", "parallel"),
        name="window_attention",
    )(qkv, qkv, qkv, _sink_rows(sink, SWA_KV_HEADS, blk))


def _ctx_kernel(*refs, gq, nctx, has_sink):
    if has_sink:
        q_ref, k_ref, v_ref, sink_ref, o_ref = refs
    else:
        q_ref, k_ref, v_ref, o_ref = refs
    qs = _stack_heads(q_ref[...], gq)
    s = lax.dot_general(qs, k_ref[...], _NT, preferred_element_type=F32)
    m_i = jnp.max(s, axis=-1, keepdims=True)
    if has_sink:
        sink = sink_ref[:, 0:1]
        m_i = jnp.maximum(m_i, sink)
    p = jnp.exp(s - m_i)
    l_i = jnp.sum(p, axis=-1, keepdims=True)
    if has_sink:
        l_i = l_i + jnp.exp(sink - m_i)
    acc = jnp.dot(p.astype(MXU_DTYPE), v_ref[...], preferred_element_type=F32)
    o_ref[...] = _unstack_heads(acc / l_i, gq, nctx).astype(o_ref.dtype)


def _ctx_attention(qkv, nctx, q_slot, k_slot, v_slot, sink):
    gq = 4
    qw = gq * HEAD_DIM
    has_sink = sink is not None
    in_specs = [
        pl.BlockSpec((nctx, qw), lambda h: (0, q_slot // gq + h)),
        pl.BlockSpec((nctx, HEAD_DIM), lambda h: (0, k_slot + h)),
        pl.BlockSpec((nctx, HEAD_DIM), lambda h: (0, v_slot + h)),
    ]
    args = [qkv, qkv, qkv]
    if has_sink:
        in_specs.append(pl.BlockSpec((None, gq * nctx, 128), lambda h: (h, 0, 0)))
        args.append(_sink_rows(sink, 2, nctx))
    return pl.pallas_call(
        functools.partial(_ctx_kernel, gq=gq, nctx=nctx, has_sink=has_sink),
        grid=(2,),
        in_specs=in_specs,
        out_specs=pl.BlockSpec((nctx, qw), lambda h: (0, h)),
        out_shape=jax.ShapeDtypeStruct((nctx, 2 * qw), MXU_DTYPE),
        compiler_params=_params("parallel"),
        name="context_attention",
    )(*args)


def _top_rows(s, count):
    rows = []
    for _ in range(count):
        mx = jnp.max(s, axis=0, keepdims=True)
        rows.append(mx)
        s = jnp.where(s == mx, NEG_BIG, s)
    return rows


def _route_kernel(q_ref, keys_ref, o_ref, *, n_keys):
    tops = []
    for half in range(2):
        qh = q_ref[:, half * PEER_HALF:(half + 1) * PEER_HALF]
        s_t = lax.dot_general(keys_ref[half], qh, _NT, preferred_element_type=F32)
        tops.append(_top_rows(s_t[0:n_keys, :], PEER_TOPK + 1))
    a_top, b_top = tops
    b_all = jnp.concatenate(b_top[:PEER_TOPK], axis=0)
    cand = jnp.concatenate([a_i + b_all for a_i in a_top[:PEER_TOPK]], axis=0)
    c = _top_rows(cand, PEER_TOPK + 1)
    runner_up = jnp.maximum(c[PEER_TOPK], jnp.maximum(a_top[PEER_TOPK] + b_top[0],
                                                       a_top[0] + b_top[PEER_TOPK]))
    thr = 0.5 * (c[PEER_TOPK - 1] + runner_up)
    z = jnp.zeros_like(thr)
    for kk in range(PEER_TOPK):
        z = z + jnp.exp(c[kk] - c[0])
    pad = jnp.zeros((5, thr.shape[1]), F32)
    o_ref[...] = jnp.concatenate([thr, c[0], 1.0 / z, pad], axis=0)


def _peer_route(q, keys_pad, n_keys):
    m = q.shape[0]
    tm = _pick(m, (256, 128))
    return pl.pallas_call(
        functools.partial(_route_kernel, n_keys=n_keys),
        grid=(m // tm, PEER_HEADS),
        in_specs=[
            pl.BlockSpec((tm, 2 * PEER_HALF), lambda i, h: (i, h)),
            pl.BlockSpec((None, 2, 128, PEER_HALF), lambda i, h: (h, 0, 0, 0)),
        ],
        out_specs=pl.BlockSpec((None, 8, tm), lambda i, h: (h, 0, i)),
        out_shape=jax.ShapeDtypeStruct((PEER_HEADS, 8, m), F32),
        compiler_params=_params("parallel", "parallel"),
        name="peer_route",
    )(q, keys_pad)


def _peer_w_kernel(h_ref, u_ref, q_ref, kx_ref, r_ref, o_ref):
    a = lax.dot_general(h_ref[...], u_ref[...], _NT, preferred_element_type=F32)
    act = 0.5 * a * (1.0 + lax.erf(a * (2.0 ** -0.5)))
    g = jnp.zeros_like(a)
    kd = 2 * PEER_HALF
    for h in range(PEER_HEADS):
        s = jnp.dot(q_ref[:, h * kd:(h + 1) * kd], kx_ref[h], preferred_element_type=F32)
        thr = r_ref[:, h:h + 1]
        mx = r_ref[:, PEER_HEADS + h:PEER_HEADS + h + 1]
        inv_z = r_ref[:, 2 * PEER_HEADS + h:2 * PEER_HEADS + h + 1]
        g = g + jnp.where(s >= thr, jnp.exp(s - mx) * inv_z, 0.0)
    o_ref[...] = (g * act).astype(o_ref.dtype)


def _peer_weights(h2, u, q, kexp, route):
    m, d = h2.shape
    n_exp = u.shape[0]
    tm = _pick(m, (640, 256, 128))
    te = _pick(n_exp, (640, 512, 256, 128))
    return pl.pallas_call(
        _peer_w_kernel,
        grid=(n_exp // te, m // tm),
        in_specs=[
            pl.BlockSpec((tm, d), lambda e, i: (i, 0)),
            pl.BlockSpec((te, d), lambda e, i: (e, 0)),
            pl.BlockSpec((tm, q.shape[1]), lambda e, i: (i, 0)),
            pl.BlockSpec((PEER_HEADS, 2 * PEER_HALF, te), lambda e, i: (0, 0, e)),
            pl.BlockSpec((tm, route.shape[1]), lambda e, i: (i, 0)),
        ],
        out_specs=pl.BlockSpec((tm, te), lambda e, i: (i, e)),
        out_shape=jax.ShapeDtypeStruct((m, n_exp), MXU_DTYPE),
        compiler_params=_params("parallel", "parallel"),
        name="peer_expert_weights",
    )(h2, u, q, kexp, route)


def _peer_key_tables(keys):
    nh, _, nk, kd = keys.shape
    keys_pad = jnp.zeros((nh, 2, 128, kd), keys.dtype).at[:, :, :nk].set(keys)
    k1 = jnp.repeat(keys[:, 0], nk, axis=1)
    k2 = jnp.tile(keys[:, 1], (1, nk, 1))
    kexp = jnp.swapaxes(jnp.concatenate([k1, k2], axis=-1), 1, 2)
    return keys_pad.astype(MXU_DTYPE), kexp.astype(MXU_DTYPE)


def kernel(x, c, ctx, c_ctx, ada_w, ada_b, norm1_g, w_in, conv_w, conv_b, dt_bias, a_log, d_skip,
           ssd_norm_g, swa_sink, q_norm_g, k_norm_g, w_out, norm2_g, peer_wq, peer_keys, peer_u,
           peer_v, final_g):
    assert x.shape[0] == 1 and c.shape[0] == 1 and ctx.shape[0] == 1
    s, d = x.shape[1], x.shape[2]
    nctx = ctx.shape[1]
    depth = ada_w.shape[0]
    n_heads = d_skip.shape[1]
    n_xs = n_heads * SSD_HEAD_DIM
    n_conv = conv_w.shape[2]
    n_keys = peer_keys.shape[3]
    assert n_xs == SSD_GROUPS * 4 * SSD_HEAD_DIM and n_conv == n_xs + 2 * SSD_GROUPS * SSD_STATE
    assert nctx % 256 == 0 and s % 256 == 0 and peer_keys.shape[4] == PEER_HALF
    bf = MXU_DTYPE

    cvec = jnp.zeros((8, d), F32).at[0].set(c_ctx).at[1].set(c[0])
    mods = _mods(cvec, ada_w, ada_b)[:, 0:2].reshape(depth, 2, 6, d)
    cos, sin = _rope_tables(s, nctx)
    xs = jnp.concatenate([ctx[0], x[0]], axis=0)

    o_z, o_xbc, o_dt = 0, n_xs, n_xs + n_conv
    o_att = o_dt + 2 * n_heads
    pad_dt = jnp.zeros((d, 128 - 2 * n_heads), F32)

    for l in range(depth):
        ml = [mods[l, :, i] for i in range(6)]
        w = w_in[l]
        w_a = jnp.concatenate([w[:, o_xbc:o_dt], w[:, o_z:o_xbc]], axis=1).astype(bf)
        w_b = jnp.concatenate([w[:, o_att:], w[:, o_dt:o_att], pad_dt], axis=1).astype(bf)

        h1 = _normmod(xs, norm1_g[l], ml[0], ml[1], nctx)
        zx = _matmul(h1, w_a, F32, name="in_proj_ssd")
        p2 = _matmul(h1, w_b, F32, name="in_proj_attn")

        conv_w8 = jnp.zeros((8, n_conv), F32).at[:CONV_K].set(conv_w[l])
        zpad = jnp.zeros((128 - 2 * n_heads,), F32)
        dtb = jnp.concatenate([dt_bias[l].reshape(-1), zpad]).reshape(1, 128)
        a_vec = jnp.concatenate([-jnp.exp(a_log[l].reshape(-1)), zpad]).reshape(1, 128)
        dsk = jnp.repeat(d_skip[l], SSD_HEAD_DIM).reshape(1, n_xs)
        ng = ssd_norm_g[l].reshape(1, n_xs)
        cb = conv_b[l].reshape(1, n_conv)
        yf = _ssd(zx, p2, conv_w8, cb, dtb, a_vec, dsk, ng, nctx, reverse=False)
        ssd_out = _ssd(zx, p2, conv_w8, cb, dtb, a_vec, dsk, ng, nctx, reverse=True, yf=yf)

        qkv = _qkv_prep(p2, cos, sin, q_norm_g[l], k_norm_g[l])
        swa_l = _swa_attention(qkv, swa_sink[l], nctx)
        glb_l = _glb_attention(qkv, nctx)
        swa_c = _ctx_attention(qkv, nctx, 0, 8, 10, swa_sink[l])
        glb_c = _ctx_attention(qkv, nctx, 12, 20, 22, None)
        mix = jnp.concatenate([ssd_out,
                               jnp.concatenate([swa_c, swa_l], axis=0),
                               jnp.concatenate([glb_c, glb_l], axis=0)], axis=1)
        xs = _matmul(mix, w_out[l].astype(bf), F32, res=xs, gate=ml[2], nctx=nctx, name="out_proj")

        h2 = _normmod(xs, norm2_g[l], ml[3], ml[4], nctx)
        q = _matmul(h2, peer_wq[l].astype(bf), bf, name="peer_query")
        keys_pad, kexp = _peer_key_tables(peer_keys[l])
        route = _peer_route(q, keys_pad, n_keys)
        route = jnp.transpose(route[:, 0:3], (2, 1, 0)).reshape(nctx + s, 3 * PEER_HEADS)
        wts = _peer_weights(h2, peer_u[l].astype(bf), q, kexp, route)
        xs = _matmul(wts, peer_v[l].astype(bf), F32, res=xs, gate=ml[5], nctx=nctx, name="peer_out")

    return _finalnorm(xs, final_g, nctx)[None]
```

```python
import functools
import math

import jax
import jax.numpy as jnp
import numpy as np
from jax import lax
from jax.experimental import pallas as pl
from jax.experimental.pallas import tpu as pltpu

F32 = jnp.float32
MXU_DTYPE = jnp.bfloat16

HEAD_DIM = 128
GRID_W = 64
ROPE_THETA = 10000.0
NORM_EPS = 1e-6

SSD_HEAD_DIM = 64
SSD_GROUPS = 8
SSD_STATE = 128
SSD_CHUNK = 128
CONV_K = 5
CONV_HALO = 8

SWA_KV_HEADS = 2
GLB_KV_HEADS = 2
ATT_BLOCK = 128

PEER_HEADS = 8
PEER_HALF = 128
PEER_TOPK = 16

NEG_BIG = -1e30
LOG2_E = 1.4426950408889634
VMEM_LIMIT = 56 * 1024 * 1024


def _pick(n, candidates):
    for c in candidates:
        if n % c == 0:
            return c
    raise ValueError(f"no tile in {candidates} divides {n}")


def _params(*sem):
    return pltpu.CompilerParams(dimension_semantics=sem, vmem_limit_bytes=VMEM_LIMIT)


def _row_select(ref, row0, tm, nctx):
    rows = row0 + lax.broadcasted_iota(jnp.int32, (tm, 1), 0)
    return jnp.where(rows < nctx, ref[0:1, :], ref[1:2, :])


def _mods_kernel(c_ref, w_ref, b_ref, o_ref):
    cv = c_ref[...]
    sv = cv * (1.0 / (1.0 + jnp.exp(-cv)))
    acc = jnp.dot(sv.astype(MXU_DTYPE), w_ref[...].astype(MXU_DTYPE),
                  preferred_element_type=F32)
    o_ref[...] = acc + b_ref[...]


def _mods(cvec, ada_w, ada_b):
    depth, d, n = ada_w.shape
    tn = _pick(n, (512, 256, 128))
    return pl.pallas_call(
        _mods_kernel,
        grid=(depth, n // tn),
        in_specs=[
            pl.BlockSpec((8, d), lambda l, j: (0, 0)),
            pl.BlockSpec((None, d, tn), lambda l, j: (l, 0, j)),
            pl.BlockSpec((None, 1, tn), lambda l, j: (l, 0, j)),
        ],
        out_specs=pl.BlockSpec((None, 8, tn), lambda l, j: (l, 0, j)),
        out_shape=jax.ShapeDtypeStruct((depth, 8, n), F32),
        compiler_params=_params("parallel", "parallel"),
        name="adaln_mods",
    )(cvec, ada_w, ada_b.reshape(depth, 1, n))


def _normmod_kernel(x_ref, g_ref, shift_ref, scale_ref, o_ref, *, tm, nctx):
    x = x_ref[...]
    y = x * lax.rsqrt(jnp.mean(x * x, axis=-1, keepdims=True) + NORM_EPS)
    y = y * g_ref[...]
    row0 = pl.program_id(0) * tm
    scale = _row_select(scale_ref, row0, tm, nctx)
    shift = _row_select(shift_ref, row0, tm, nctx)
    o_ref[...] = (y * (1.0 + scale) + shift).astype(o_ref.dtype)


def _normmod(xs, g, shift, scale, nctx):
    m, d = xs.shape
    tm = _pick(m, (256, 128))
    return pl.pallas_call(
        functools.partial(_normmod_kernel, tm=tm, nctx=nctx),
        grid=(m // tm,),
        in_specs=[
            pl.BlockSpec((tm, d), lambda i: (i, 0)),
            pl.BlockSpec((1, d), lambda i: (0, 0)),
            pl.BlockSpec((2, d), lambda i: (0, 0)),
            pl.BlockSpec((2, d), lambda i: (0, 0)),
        ],
        out_specs=pl.BlockSpec((tm, d), lambda i: (i, 0)),
        out_shape=jax.ShapeDtypeStruct((m, d), MXU_DTYPE),
        compiler_params=_params("parallel"),
        name="norm_modulate",
    )(xs, g.reshape(1, d), shift, scale)


def _finalnorm_kernel(x_ref, g_ref, o_ref):
    x = x_ref[...]
    y = x * lax.rsqrt(jnp.mean(x * x, axis=-1, keepdims=True) + NORM_EPS)
    o_ref[...] = y * g_ref[...]


def _finalnorm(xs, g, nctx):
    m, d = xs.shape
    s = m - nctx
    tm = _pick(math.gcd(s, nctx), (256, 128))
    off = nctx // tm
    return pl.pallas_call(
        _finalnorm_kernel,
        grid=(s // tm,),
        in_specs=[
            pl.BlockSpec((tm, d), lambda i: (i + off, 0)),
            pl.BlockSpec((1, d), lambda i: (0, 0)),
        ],
        out_specs=pl.BlockSpec((tm, d), lambda i: (i, 0)),
        out_shape=jax.ShapeDtypeStruct((s, d), F32),
        compiler_params=_params("parallel"),
        name="final_norm",
    )(xs, g.reshape(1, d))


def _mm_kernel(a_ref, b_ref, o_ref):
    o_ref[...] = jnp.dot(a_ref[...], b_ref[...],
                         preferred_element_type=F32).astype(o_ref.dtype)


def _mm_res_kernel(a_ref, b_ref, res_ref, gate_ref, o_ref, *, tm, nctx):
    acc = jnp.dot(a_ref[...], b_ref[...], preferred_element_type=F32)
    gate = _row_select(gate_ref, pl.program_id(1) * tm, tm, nctx)
    o_ref[...] = res_ref[...] + gate * acc


def _matmul(a, b, out_dtype, *, res=None, gate=None, nctx=0, name="matmul"):
    m, k = a.shape
    n = b.shape[1]
    tm = _pick(m, (640, 256, 128))
    tn = _pick(n, (512, 640, 256, 128))
    a_spec = pl.BlockSpec((tm, k), lambda j, i: (i, 0))
    b_spec = pl.BlockSpec((k, tn), lambda j, i: (0, j))
    o_spec = pl.BlockSpec((tm, tn), lambda j, i: (i, j))
    if res is None:
        kern, specs, args = _mm_kernel, [a_spec, b_spec], (a, b)
    else:
        kern = functools.partial(_mm_res_kernel, tm=tm, nctx=nctx)
        specs = [a_spec, b_spec, o_spec, pl.BlockSpec((2, tn), lambda j, i: (0, j))]
        args = (a, b, res, gate)
    return pl.pallas_call(
        kern,
        grid=(n // tn, m // tm),
        in_specs=specs,
        out_specs=o_spec,
        out_shape=jax.ShapeDtypeStruct((m, n), out_dtype),
        compiler_params=_params("parallel", "parallel"),
        name=name,
    )(*args)


def _mm_parts_res_kernel(*refs, n_parts, tm, nctx):
    a_refs, b_refs = refs[:n_parts], refs[n_parts:2 * n_parts]
    res_ref, gate_ref, o_ref = refs[2 * n_parts:]
    acc = jnp.dot(a_refs[0][...], b_refs[0][...], preferred_element_type=F32)
    for a_ref, b_ref in zip(a_refs[1:], b_refs[1:]):
        acc = acc + jnp.dot(a_ref[...], b_ref[...], preferred_element_type=F32)
    gate = _row_select(gate_ref, pl.program_id(1) * tm, tm, nctx)
    o_ref[...] = res_ref[...] + gate * acc


def _matmul_parts(parts, b, *, res, gate, nctx, name):
    m = parts[0].shape[0]
    n = b.shape[1]
    tm = _pick(m, (640, 256, 128))
    tn = _pick(n, (512, 256, 128))
    a_specs, b_specs, off = [], [], 0
    for p in parts:
        kp = p.shape[1]
        assert off % kp == 0, "each part's rows of b must start on a multiple of its width"
        a_specs.append(pl.BlockSpec((tm, kp), lambda j, i: (i, 0)))
        b_specs.append(pl.BlockSpec((kp, tn), functools.partial(
            lambda j, i, blk: (blk, j), blk=off // kp)))
        off += kp
    assert off == b.shape[0]
    o_spec = pl.BlockSpec((tm, tn), lambda j, i: (i, j))
    return pl.pallas_call(
        functools.partial(_mm_parts_res_kernel, n_parts=len(parts), tm=tm, nctx=nctx),
        grid=(n // tn, m // tm),
        in_specs=a_specs + b_specs + [o_spec, pl.BlockSpec((2, tn), lambda j, i: (0, j))],
        out_specs=o_spec,
        out_shape=jax.ShapeDtypeStruct((m, n), F32),
        compiler_params=_params("parallel", "parallel"),
        name=name,
    )(*parts, *([b] * len(parts)), res, gate)


def _ssd_chunk_of_step(s, ncc, nch, reverse):
    if not reverse:
        return s
    return jnp.where(s < ncc, ncc - 1 - s, nch - 1 + ncc - s)


def _ssd_conv_kernel(xbc_ref, prev_ref, next_ref, cw_ref, cb_ref, o_ref, xw_ref, *, ncc, nch):
    q = SSD_CHUNK
    c = pl.program_id(0)
    has_prev = jnp.logical_and(c != 0, c != ncc)
    has_next = jnp.logical_and(c != ncc - 1, c != nch - 1)
    xw_ref[0:CONV_HALO, :] = jnp.where(has_prev, prev_ref[...], 0.0)
    xw_ref[CONV_HALO:CONV_HALO + q, :] = xbc_ref[...]
    xw_ref[CONV_HALO + q:, :] = jnp.where(has_next, next_ref[...], 0.0)
    u = jnp.zeros(o_ref.shape, F32) + cb_ref[...]
    for kk in range(CONV_K):
        start = CONV_HALO - CONV_K // 2 + kk
        u = u + xw_ref[start:start + q, :] * cw_ref[kk:kk + 1, :]
    o_ref[...] = u * (1.0 / (1.0 + jnp.exp(-u)))


def _ssd_conv(zx, conv_w8, conv_b, nctx):
    m = zx.shape[0]
    q = SSD_CHUNK
    nch, ncc = m // q, nctx // q
    n_conv = conv_b.shape[1]
    hb = q // CONV_HALO
    n_hb = m // CONV_HALO
    return pl.pallas_call(
        functools.partial(_ssd_conv_kernel, ncc=ncc, nch=nch),
        grid=(nch,),
        in_specs=[
            pl.BlockSpec((q, n_conv), lambda c: (c, 0)),
            pl.BlockSpec((CONV_HALO, n_conv), lambda c: (jnp.maximum(c * hb - 1, 0), 0)),
            pl.BlockSpec((CONV_HALO, n_conv), lambda c: (jnp.minimum((c + 1) * hb, n_hb - 1), 0)),
            pl.BlockSpec((8, n_conv), lambda c: (0, 0)),
            pl.BlockSpec((1, n_conv), lambda c: (0, 0)),
        ],
        out_specs=pl.BlockSpec((q, n_conv), lambda c: (c, 0)),
        out_shape=jax.ShapeDtypeStruct((m, n_conv), F32),
        scratch_shapes=[pltpu.VMEM((q + 2 * CONV_HALO, n_conv), F32)],
        compiler_params=_params("parallel"),
        name="ssd_conv",
    )(zx, zx, zx, conv_w8, conv_b)


def _ssd_kernel(*refs, reverse, ncc, nch):
    if reverse:
        (u_ref, dt_ref, dtb_ref, a_ref, z_ref, yf_ref, dsk_ref, ng_ref, o_ref, h_ref) = refs
    else:
        (u_ref, dt_ref, dtb_ref, a_ref, o_ref, h_ref) = refs
    q = SSD_CHUNK
    step = pl.program_id(0)
    n_xs = SSD_GROUPS * 4 * SSD_HEAD_DIM
    n_bc = SSD_GROUPS * SSD_STATE
    n_heads = n_xs // SSD_HEAD_DIM

    @pl.when(step == 0)
    def _():
        h_ref[...] = jnp.zeros_like(h_ref)

    u = u_ref[...]

    dtr = dt_ref[...] + dtb_ref[...]
    dt = jnp.maximum(dtr, 0.0) + jnp.log1p(jnp.exp(-jnp.abs(dtr)))
    d_a = dt * a_ref[...]
    ii = lax.broadcasted_iota(jnp.int32, (q, q), 0)
    jj = lax.broadcasted_iota(jnp.int32, (q, q), 1)
    mask = (jj >= ii) if reverse else (jj <= ii)
    tri = jnp.where(mask, 1.0, 0.0).astype(F32)
    cs = jnp.dot(tri, d_a, preferred_element_type=F32, precision=lax.Precision.HIGHEST)
    cs_t = cs.T
    dt_t = dt.T
    last = 0 if reverse else q - 1
    total = cs[last:last + 1, :]
    w_state_t = (dt * jnp.exp(total - cs)).T
    e_cs = jnp.exp(cs)
    chunk_decay = jnp.exp(total)
    lane = lax.broadcasted_iota(jnp.int32, (1, 2 * SSD_HEAD_DIM), 1)
    first_half = lane < SSD_HEAD_DIM
    dir_off = n_heads if reverse else 0

    y_parts = []
    for g in range(SSD_GROUPS):
        bg = u[:, n_xs + g * SSD_STATE:n_xs + (g + 1) * SSD_STATE]
        cg = u[:, n_xs + n_bc + g * SSD_STATE:n_xs + n_bc + (g + 1) * SSD_STATE]
        cb = lax.dot_general(cg.astype(MXU_DTYPE), bg.astype(MXU_DTYPE),
                             (((1,), (1,)), ((), ())), preferred_element_type=F32)
        bg_t = bg.T
        for pair in range(2):
            lo = g * 4 * SSD_HEAD_DIM + pair * 2 * SSD_HEAD_DIM
            xs_pair = u[:, lo:lo + 2 * SSD_HEAD_DIM].astype(MXU_DTYPE)
            h_pair = h_ref[g, :, pair * 128:(pair + 1) * 128]
            rhs = jnp.concatenate([xs_pair, h_pair.astype(MXU_DTYPE)], axis=0)
            ys, sts, decs = [], [], []
            for r in range(2):
                col = dir_off + g * 4 + pair * 2 + r
                seg = jnp.exp(jnp.where(mask, cs[:, col:col + 1] - cs_t[col:col + 1, :], -jnp.inf))
                m_h = cb * seg * dt_t[col:col + 1, :]
                c_h = cg * e_cs[:, col:col + 1]
                lhs = jnp.concatenate([m_h.astype(MXU_DTYPE), c_h.astype(MXU_DTYPE)], axis=1)
                ys.append(jnp.dot(lhs, rhs, preferred_element_type=F32))
                b_h = bg_t * w_state_t[col:col + 1, :]
                sts.append(jnp.dot(b_h.astype(MXU_DTYPE), xs_pair, preferred_element_type=F32))
                decs.append(chunk_decay[:, col:col + 1])
            y_parts.append(jnp.where(first_half, ys[0], ys[1]))
            dec = jnp.where(first_half, decs[0], decs[1])
            h_ref[g, :, pair * 128:(pair + 1) * 128] = (
                h_pair * dec + jnp.where(first_half, sts[0], sts[1]))
    y = jnp.concatenate(y_parts, axis=1)

    if not reverse:
        o_ref[...] = y
    else:
        y = y + yf_ref[...] + dsk_ref[...] * u[:, :n_xs]
        zz = z_ref[...]
        y = y * (zz * (1.0 / (1.0 + jnp.exp(-zz))))
        gw = n_xs // SSD_GROUPS
        outs = []
        for g in range(SSD_GROUPS):
            sl = y[:, g * gw:(g + 1) * gw]
            outs.append(sl * lax.rsqrt(jnp.mean(sl * sl, axis=-1, keepdims=True) + NORM_EPS))
        o_ref[...] = (jnp.concatenate(outs, axis=1) * ng_ref[...]).astype(o_ref.dtype)


def _ssd(u, zx, p2, dtb, a_vec, dsk, ng, nctx, *, reverse, yf=None):
    m = zx.shape[0]
    q = SSD_CHUNK
    nch, ncc = m // q, nctx // q
    n_xs = SSD_GROUPS * 4 * SSD_HEAD_DIM
    n_conv = u.shape[1]
    dt_blk = (p2.shape[1] - 128) // 128

    def cidx(s):
        return _ssd_chunk_of_step(s, ncc, nch, reverse)

    in_specs = [
        pl.BlockSpec((q, n_conv), lambda s: (cidx(s), 0)),
        pl.BlockSpec((q, 128), lambda s: (cidx(s), dt_blk)),
        pl.BlockSpec((1, 128), lambda s: (0, 0)),
        pl.BlockSpec((1, 128), lambda s: (0, 0)),
    ]
    args = [u, p2, dtb, a_vec]
    if reverse:
        in_specs += [
            pl.BlockSpec((q, n_xs), lambda s: (cidx(s), n_conv // n_xs)),
            pl.BlockSpec((q, n_xs), lambda s: (cidx(s), 0)),
            pl.BlockSpec((1, n_xs), lambda s: (0, 0)),
            pl.BlockSpec((1, n_xs), lambda s: (0, 0)),
        ]
        args += [zx, yf, dsk, ng]
    return pl.pallas_call(
        functools.partial(_ssd_kernel, reverse=reverse, ncc=ncc, nch=nch),
        grid=(nch,),
        in_specs=in_specs,
        out_specs=pl.BlockSpec((q, n_xs), lambda s: (cidx(s), 0)),
        out_shape=jax.ShapeDtypeStruct((m, n_xs), MXU_DTYPE if reverse else F32),
        scratch_shapes=[pltpu.VMEM((SSD_GROUPS, SSD_STATE, 4 * SSD_HEAD_DIM), F32)],
        compiler_params=_params("arbitrary"),
        name="ssd_bwd_finish" if reverse else "ssd_fwd",
    )(*args)


_QKV_SLOTS = (
    (8, None, True, True),
    (2, None, True, False),
    (2, None, False, False),
    (8, "q", True, True),
    (2, "k", True, False),
    (2, None, False, False),
)
N_QKV = sum(s[0] for s in _QKV_SLOTS) * HEAD_DIM


GLB_V_SLOT = 22
VT_ROWS = HEAD_DIM + 16
KV_CHUNK = 256


def _qkv_kernel(p_ref, cos_ref, sin_ref, qg_ref, kg_ref, o_ref, vt_ref):
    tm = p_ref.shape[0]
    row = lax.broadcasted_iota(jnp.int32, (VT_ROWS - HEAD_DIM, tm), 0)
    ones_rows = jnp.where(row == 0, 1.0, 0.0).astype(vt_ref.dtype)
    for h in range(GLB_KV_HEADS):
        v = p_ref[:, (GLB_V_SLOT + h) * HEAD_DIM:(GLB_V_SLOT + h + 1) * HEAD_DIM]
        vt_ref[h, 0, 0:HEAD_DIM, :] = v.T.astype(vt_ref.dtype)
        vt_ref[h, 0, HEAD_DIM:VT_ROWS, :] = ones_rows
    cos = cos_ref[...]
    sin = sin_ref[...]
    lane = lax.broadcasted_iota(jnp.int32, (1, HEAD_DIM), 1)
    lower = (lane % (HEAD_DIM // 2)) < (HEAD_DIM // 4)
    slot = 0
    for count, norm, rope, scale in _QKV_SLOTS:
        for _ in range(count):
            t = p_ref[:, slot * HEAD_DIM:(slot + 1) * HEAD_DIM]
            if norm is not None:
                g = qg_ref[...] if norm == "q" else kg_ref[...]
                t = t * lax.rsqrt(jnp.mean(t * t, axis=-1, keepdims=True) + NORM_EPS) * g
            if rope:
                swapped = jnp.where(lower, pltpu.roll(t, HEAD_DIM - HEAD_DIM // 4, 1),
                                    pltpu.roll(t, HEAD_DIM // 4, 1))
                t = t * cos + swapped * sin
            if scale:
                t = t * (HEAD_DIM ** -0.5 * LOG2_E)
            o_ref[:, slot * HEAD_DIM:(slot + 1) * HEAD_DIM] = t.astype(o_ref.dtype)
            slot += 1


def _qkv_prep(p2, cos, sin, qg, kg):
    m = p2.shape[0]
    tm = KV_CHUNK
    return pl.pallas_call(
        _qkv_kernel,
        grid=(m // tm,),
        in_specs=[
            pl.BlockSpec((tm, N_QKV), lambda i: (i, 0)),
            pl.BlockSpec((tm, HEAD_DIM), lambda i: (i, 0)),
            pl.BlockSpec((tm, HEAD_DIM), lambda i: (i, 0)),
            pl.BlockSpec((1, HEAD_DIM), lambda i: (0, 0)),
            pl.BlockSpec((1, HEAD_DIM), lambda i: (0, 0)),
        ],
        out_specs=[
            pl.BlockSpec((tm, N_QKV), lambda i: (i, 0)),
            pl.BlockSpec((GLB_KV_HEADS, 1, VT_ROWS, tm), lambda i: (0, i, 0, 0)),
        ],
        out_shape=[
            jax.ShapeDtypeStruct((m, N_QKV), MXU_DTYPE),
            jax.ShapeDtypeStruct((GLB_KV_HEADS, m // tm, VT_ROWS, tm), MXU_DTYPE),
        ],
        compiler_params=_params("parallel"),
        name="qkv_prep",
    )(p2, cos, sin, qg.reshape(1, HEAD_DIM), kg.reshape(1, HEAD_DIM))


def _rope_tables(s, nctx):
    t = jnp.arange(s, dtype=jnp.int32)
    pos_r = (t // GRID_W).astype(F32)
    pos_c = (t % GRID_W).astype(F32)
    n_freq = HEAD_DIM // 4
    inv = ROPE_THETA ** (-jnp.arange(n_freq, dtype=F32) / n_freq)
    ar, ac = pos_r[:, None] * inv, pos_c[:, None] * inv
    cos = jnp.concatenate([jnp.cos(ar), jnp.cos(ar), jnp.cos(ac), jnp.cos(ac)], axis=1)
    sin = jnp.concatenate([-jnp.sin(ar), jnp.sin(ar), -jnp.sin(ac), jnp.sin(ac)], axis=1)
    cos = jnp.concatenate([jnp.ones((nctx, HEAD_DIM), F32), cos], axis=0)
    sin = jnp.concatenate([jnp.zeros((nctx, HEAD_DIM), F32), sin], axis=0)
    return cos, sin


def _stack_heads(q, n):
    return jnp.concatenate([q[:, g * HEAD_DIM:(g + 1) * HEAD_DIM] for g in range(n)], axis=0)


def _unstack_heads(o, n, rows):
    return jnp.concatenate([o[g * rows:(g + 1) * rows, :] for g in range(n)], axis=1)


_NT = (((1,), (1,)), ((), ()))


def _glb_kernel(q_ref, k_ref, vt_ref, o_ref, acc_ref, *, tq, nk, unroll, gq):
    tk = KV_CHUNK
    acc_ref[...] = jnp.zeros_like(acc_ref)
    qs = [q_ref[:, g * HEAD_DIM:(g + 1) * HEAD_DIM] for g in range(gq)]

    def body(jj, ms):
        ms = list(ms)

        def scores(c):
            start = pl.multiple_of((jj * unroll + c) * tk, tk)
            kb = k_ref[pl.ds(start, tk), :]
            return [lax.dot_general(kb, q, _NT, preferred_element_type=F32) for q in qs]

        def softmax(s_ts):
            alphas, p_ts = [], []
            for g in range(gq):
                m_new = jnp.maximum(ms[g], jnp.max(s_ts[g], axis=0, keepdims=True))
                alphas.append(jnp.exp2(ms[g] - m_new))
                p_ts.append(jnp.exp2(s_ts[g] - m_new).astype(MXU_DTYPE))
                ms[g] = m_new
            return alphas, p_ts

        def accumulate(c, alphas, p_ts):
            vtb = vt_ref[jj * unroll + c]
            pvs = [jnp.dot(vtb, p_t, preferred_element_type=F32) for p_t in p_ts]
            for g in range(gq):
                acc_ref[g] = alphas[g] * acc_ref[g] + pvs[g]

        s_cur = scores(0)
        s_next = scores(1) if unroll > 1 else None
        prob = softmax(s_cur)
        for c in range(1, unroll):
            s_cur = s_next
            s_next = scores(c + 1) if c + 1 < unroll else None
            accumulate(c - 1, *prob)
            prob = softmax(s_cur)
        accumulate(unroll - 1, *prob)
        return tuple(ms)

    init = tuple(jnp.full((1, tq), NEG_BIG, F32) for _ in range(gq))
    lax.fori_loop(0, nk // unroll, body, init)
    for g in range(gq):
        acc = acc_ref[g]
        out_t = acc[0:HEAD_DIM, :] / acc[HEAD_DIM:HEAD_DIM + 1, :]
        o_ref[:, g * HEAD_DIM:(g + 1) * HEAD_DIM] = out_t.T.astype(o_ref.dtype)


def _glb_attention(qkv, vt, nctx):
    m = qkv.shape[0]
    s = m - nctx
    gq = 8 // GLB_KV_HEADS
    tq = KV_CHUNK
    qw = gq * HEAD_DIM
    q_blk0 = (12 * HEAD_DIM) // qw
    nk = m // KV_CHUNK
    unroll = _pick(nk, (13, 5, 3, 2, 1))
    return pl.pallas_call(
        functools.partial(_glb_kernel, tq=tq, nk=nk, unroll=unroll, gq=gq),
        grid=(GLB_KV_HEADS, s // tq),
        in_specs=[
            pl.BlockSpec((tq, qw), lambda h, i: (i + nctx // tq, q_blk0 + h)),
            pl.BlockSpec((m, HEAD_DIM), lambda h, i: (0, 20 + h)),
            pl.BlockSpec((None, nk, VT_ROWS, KV_CHUNK), lambda h, i: (h, 0, 0, 0)),
        ],
        out_specs=pl.BlockSpec((tq, qw), lambda h, i: (i + nctx // tq, h)),
        out_shape=jax.ShapeDtypeStruct((m, GLB_KV_HEADS * qw), MXU_DTYPE),
        scratch_shapes=[pltpu.VMEM((gq, VT_ROWS, tq), F32)],
        compiler_params=_params("parallel", "parallel"),
        name="global_attention",
    )(qkv, qkv, vt)


def _swa_kernel(q_ref, k_ref, v_ref, sink_ref, o_ref, *, nctx, nb, gq):
    blk = ATT_BLOCK
    n = pl.program_id(1)
    qs = _stack_heads(q_ref[...], gq)
    rows = gq * blk
    base = nctx + n * blk
    p_start = pl.multiple_of(base - blk, blk)
    c_start = pl.multiple_of(base, blk)
    n_start = pl.multiple_of(jnp.minimum(base + blk, nctx + (nb - 1) * blk), blk)
    ri = lax.broadcasted_iota(jnp.int32, (rows, blk), 0) % blk
    cj = lax.broadcasted_iota(jnp.int32, (rows, blk), 1)

    def scores(start, size):
        return lax.dot_general(qs, k_ref[pl.ds(start, size), :], _NT, preferred_element_type=F32)

    s_prev = jnp.where(jnp.logical_and(cj >= ri, n > 0), scores(p_start, blk), NEG_BIG)
    s_cur = scores(c_start, blk)
    s_next = jnp.where(jnp.logical_and(cj <= ri, n < nb - 1), scores(n_start, blk), NEG_BIG)
    s_ctx = scores(0, nctx)
    sink = sink_ref[:, 0:1]
    m_i = sink
    for s in (s_prev, s_cur, s_next, s_ctx):
        m_i = jnp.maximum(m_i, jnp.max(s, axis=-1, keepdims=True))
    l_i = jnp.exp2(sink - m_i)
    acc = jnp.zeros((rows, HEAD_DIM), F32)
    for s, start, size in ((s_prev, p_start, blk), (s_cur, c_start, blk),
                           (s_next, n_start, blk), (s_ctx, 0, nctx)):
        p = jnp.exp2(s - m_i)
        l_i = l_i + jnp.sum(p, axis=-1, keepdims=True)
        acc = acc + jnp.dot(p.astype(MXU_DTYPE), v_ref[pl.ds(start, size), :],
                            preferred_element_type=F32)
    o_ref[...] = _unstack_heads(acc / l_i, gq, blk).astype(o_ref.dtype)


def _sink_rows(sink, kv_heads, rows_per_head):
    gq = sink.shape[0] // kv_heads
    t = jnp.repeat(sink.reshape(kv_heads, gq).astype(F32) * LOG2_E, rows_per_head, axis=1)
    return jnp.broadcast_to(t[:, :, None], (kv_heads, gq * rows_per_head, 128))


def _swa_attention(qkv, sink, nctx):
    m = qkv.shape[0]
    s = m - nctx
    blk = ATT_BLOCK
    gq = 8 // SWA_KV_HEADS
    nb = s // blk
    qw = gq * HEAD_DIM
    return pl.pallas_call(
        functools.partial(_swa_kernel, nctx=nctx, nb=nb, gq=gq),
        grid=(SWA_KV_HEADS, nb),
        in_specs=[
            pl.BlockSpec((blk, qw), lambda h, i: (i + nctx // blk, h)),
            pl.BlockSpec((m, HEAD_DIM), lambda h, i: (0, 8 + h)),
            pl.BlockSpec((m, HEAD_DIM), lambda h, i: (0, 10 + h)),
            pl.BlockSpec((None, gq * blk, 128), lambda h, i: (h, 0, 0)),
        ],
        out_specs=pl.BlockSpec((blk, qw), lambda h, i: (i + nctx // blk, h)),
        out_shape=jax.ShapeDtypeStruct((m, SWA_KV_HEADS * qw), MXU_DTYPE),
        compiler_params=_params("parallel", "parallel"),
        name="window_attention",
    )(qkv, qkv, qkv, _sink_rows(sink, SWA_KV_HEADS, blk))


def _ctx_kernel(*refs, gq, nctx, has_sink):
    if has_sink:
        q_ref, k_ref, v_ref, sink_ref, _, o_ref = refs
    else:
        q_ref, k_ref, v_ref, _, o_ref = refs
    qs = _stack_heads(q_ref[...], gq)
    s = lax.dot_general(qs, k_ref[...], _NT, preferred_element_type=F32)
    m_i = jnp.max(s, axis=-1, keepdims=True)
    if has_sink:
        sink = sink_ref[:, 0:1]
        m_i = jnp.maximum(m_i, sink)
    p = jnp.exp2(s - m_i)
    l_i = jnp.sum(p, axis=-1, keepdims=True)
    if has_sink:
        l_i = l_i + jnp.exp2(sink - m_i)
    acc = jnp.dot(p.astype(MXU_DTYPE), v_ref[...], preferred_element_type=F32)
    o_ref[...] = _unstack_heads(acc / l_i, gq, nctx).astype(o_ref.dtype)


def _ctx_attention(qkv, dst, nctx, q_slot, k_slot, v_slot, sink):
    gq = 4
    qw = gq * HEAD_DIM
    has_sink = sink is not None
    in_specs = [
        pl.BlockSpec((nctx, qw), lambda h: (0, q_slot // gq + h)),
        pl.BlockSpec((nctx, HEAD_DIM), lambda h: (0, k_slot + h)),
        pl.BlockSpec((nctx, HEAD_DIM), lambda h: (0, v_slot + h)),
    ]
    args = [qkv, qkv, qkv]
    if has_sink:
        in_specs.append(pl.BlockSpec((None, gq * nctx, 128), lambda h: (h, 0, 0)))
        args.append(_sink_rows(sink, 2, nctx))
    in_specs.append(pl.BlockSpec(memory_space=pl.ANY))
    args.append(dst)
    return pl.pallas_call(
        functools.partial(_ctx_kernel, gq=gq, nctx=nctx, has_sink=has_sink),
        grid=(2,),
        in_specs=in_specs,
        out_specs=pl.BlockSpec((nctx, qw), lambda h: (0, h)),
        out_shape=jax.ShapeDtypeStruct(dst.shape, dst.dtype),
        input_output_aliases={len(args) - 1: 0},
        compiler_params=_params("parallel"),
        name="context_attention",
    )(*args)


def _top_rows(s, count):
    rows = []
    for _ in range(count):
        mx = jnp.max(s, axis=0, keepdims=True)
        rows.append(mx)
        s = jnp.where(s == mx, NEG_BIG, s)
    return rows


def _route_kernel(q_ref, keys_ref, o_ref, *, n_keys):
    tops = []
    for half in range(2):
        qh = q_ref[:, half * PEER_HALF:(half + 1) * PEER_HALF]
        s_t = lax.dot_general(keys_ref[half], qh, _NT, preferred_element_type=F32)
        tops.append(_top_rows(s_t[0:n_keys, :], PEER_TOPK + 1))
    a_top, b_top = tops
    a_all = jnp.concatenate(a_top[:PEER_TOPK], axis=0)
    b_all = jnp.concatenate(b_top[:PEER_TOPK], axis=0)
    lo, hi = slice(0, 8), slice(8, PEER_TOPK)
    upper = lax.broadcasted_iota(jnp.int32, (8, 1), 0) >= 4
    cand = jnp.concatenate(
        [a_top[0] + b_all[lo], a_top[0] + b_all[hi], b_top[0] + a_all[hi]]
        + [a_top[i] + b_all[lo] for i in (1, 2, 3)]
        + [jnp.where(upper, b_top[j] + a_all[lo], NEG_BIG) for j in (0, 1, 2)], axis=0)
    c = _top_rows(cand, PEER_TOPK + 1)
    runner_up = jnp.maximum(c[PEER_TOPK], jnp.maximum(a_top[PEER_TOPK] + b_top[0],
                                                       a_top[0] + b_top[PEER_TOPK]))
    thr = 0.5 * (c[PEER_TOPK - 1] + runner_up)
    z = jnp.zeros_like(thr)
    for kk in range(PEER_TOPK):
        z = z + jnp.exp(c[kk] - c[0])
    pad = jnp.zeros((6, thr.shape[1]), F32)
    o_ref[...] = jnp.concatenate([thr, c[0] + jnp.log(z), pad], axis=0)


def _peer_route(q, keys_pad, n_keys):
    m = q.shape[0]
    tm = _pick(m, (256, 128))
    return pl.pallas_call(
        functools.partial(_route_kernel, n_keys=n_keys),
        grid=(m // tm, PEER_HEADS),
        in_specs=[
            pl.BlockSpec((tm, 2 * PEER_HALF), lambda i, h: (i, h)),
            pl.BlockSpec((None, 2, 128, PEER_HALF), lambda i, h: (h, 0, 0, 0)),
        ],
        out_specs=pl.BlockSpec((None, 8, tm), lambda i, h: (h, 0, i)),
        out_shape=jax.ShapeDtypeStruct((PEER_HEADS, 8, m), F32),
        compiler_params=_params("parallel", "parallel"),
        name="peer_route",
    )(q, keys_pad)


def _peer_w_kernel(h_ref, u_ref, q_ref, kx_ref, r_ref, o_ref):
    a = lax.dot_general(h_ref[...], u_ref[...], _NT, preferred_element_type=F32)
    act = 0.5 * a * (1.0 + lax.erf(a * (2.0 ** -0.5)))
    g = jnp.zeros_like(a)
    kd = 2 * PEER_HALF
    for h in range(PEER_HEADS):
        s = jnp.dot(q_ref[:, h * kd:(h + 1) * kd], kx_ref[h], preferred_element_type=F32)
        thr = r_ref[:, h:h + 1]
        log_norm = r_ref[:, PEER_HEADS + h:PEER_HEADS + h + 1]
        g = g + jnp.where(s >= thr, jnp.exp(s - log_norm), 0.0)
    o_ref[...] = (g * act).astype(o_ref.dtype)


def _peer_weights(h2, u, q, kexp, route):
    m, d = h2.shape
    n_exp = u.shape[0]
    tm = _pick(m, (256, 128))
    te = _pick(n_exp, (1280, 256, 128))
    return pl.pallas_call(
        _peer_w_kernel,
        grid=(n_exp // te, m // tm),
        in_specs=[
            pl.BlockSpec((tm, d), lambda e, i: (i, 0)),
            pl.BlockSpec((te, d), lambda e, i: (e, 0)),
            pl.BlockSpec((tm, q.shape[1]), lambda e, i: (i, 0)),
            pl.BlockSpec((PEER_HEADS, 2 * PEER_HALF, te), lambda e, i: (0, 0, e)),
            pl.BlockSpec((tm, route.shape[1]), lambda e, i: (i, 0)),
        ],
        out_specs=pl.BlockSpec((tm, te), lambda e, i: (i, e)),
        out_shape=jax.ShapeDtypeStruct((m, n_exp), MXU_DTYPE),
        compiler_params=_params("parallel", "parallel"),
        name="peer_expert_weights",
    )(h2, u, q, kexp, route)


def _peer_key_tables(keys):
    nh, _, nk, kd = keys.shape
    keys_pad = jnp.zeros((nh, 2, 128, kd), keys.dtype).at[:, :, :nk].set(keys)
    k1 = jnp.repeat(keys[:, 0], nk, axis=1)
    k2 = jnp.tile(keys[:, 1], (1, nk, 1))
    kexp = jnp.swapaxes(jnp.concatenate([k1, k2], axis=-1), 1, 2)
    return keys_pad.astype(MXU_DTYPE), kexp.astype(MXU_DTYPE)


def kernel(x, c, ctx, c_ctx, ada_w, ada_b, norm1_g, w_in, conv_w, conv_b, dt_bias, a_log, d_skip,
           ssd_norm_g, swa_sink, q_norm_g, k_norm_g, w_out, norm2_g, peer_wq, peer_keys, peer_u,
           peer_v, final_g):
    assert x.shape[0] == 1 and c.shape[0] == 1 and ctx.shape[0] == 1
    s, d = x.shape[1], x.shape[2]
    nctx = ctx.shape[1]
    depth = ada_w.shape[0]
    n_heads = d_skip.shape[1]
    n_xs = n_heads * SSD_HEAD_DIM
    n_conv = conv_w.shape[2]
    n_keys = peer_keys.shape[3]
    assert n_xs == SSD_GROUPS * 4 * SSD_HEAD_DIM and n_conv == n_xs + 2 * SSD_GROUPS * SSD_STATE
    assert nctx % 256 == 0 and s % 256 == 0 and peer_keys.shape[4] == PEER_HALF
    bf = MXU_DTYPE

    cvec = jnp.zeros((8, d), F32).at[0].set(c_ctx).at[1].set(c[0])
    mods = _mods(cvec, ada_w, ada_b)[:, 0:2].reshape(depth, 2, 6, d)
    cos, sin = _rope_tables(s, nctx)
    xs = jnp.concatenate([ctx[0], x[0]], axis=0)

    o_z, o_xbc, o_dt = 0, n_xs, n_xs + n_conv
    o_att = o_dt + 2 * n_heads
    pad_dt = jnp.zeros((d, 128 - 2 * n_heads), F32)

    for l in range(depth):
        ml = [mods[l, :, i] for i in range(6)]
        w = w_in[l]
        w_a = jnp.concatenate([w[:, o_xbc:o_dt], w[:, o_z:o_xbc]], axis=1).astype(bf)
        w_b = jnp.concatenate([w[:, o_att:], w[:, o_dt:o_att], pad_dt], axis=1).astype(bf)

        h1 = _normmod(xs, norm1_g[l], ml[0], ml[1], nctx)
        zx = _matmul(h1, w_a, F32, name="in_proj_ssd")
        p2 = _matmul(h1, w_b, F32, name="in_proj_attn")

        conv_w8 = jnp.zeros((8, n_conv), F32).at[:CONV_K].set(conv_w[l])
        zpad = jnp.zeros((128 - 2 * n_heads,), F32)
        dtb = jnp.concatenate([dt_bias[l].reshape(-1), zpad]).reshape(1, 128)
        a_vec = jnp.concatenate([-jnp.exp(a_log[l].reshape(-1)), zpad]).reshape(1, 128)
        dsk = jnp.repeat(d_skip[l], SSD_HEAD_DIM).reshape(1, n_xs)
        ng = ssd_norm_g[l].reshape(1, n_xs)
        cb = conv_b[l].reshape(1, n_conv)
        u = _ssd_conv(zx, conv_w8, cb, nctx)
        yf = _ssd(u, zx, p2, dtb, a_vec, dsk, ng, nctx, reverse=False)
        ssd_out = _ssd(u, zx, p2, dtb, a_vec, dsk, ng, nctx, reverse=True, yf=yf)

        qkv, vt = _qkv_prep(p2, cos, sin, q_norm_g[l], k_norm_g[l])
        swa_out = _swa_attention(qkv, swa_sink[l], nctx)
        swa_out = _ctx_attention(qkv, swa_out, nctx, 0, 8, 10, swa_sink[l])
        glb_out = _glb_attention(qkv, vt, nctx)
        glb_out = _ctx_attention(qkv, glb_out, nctx, 12, 20, GLB_V_SLOT, None)
        xs = _matmul_parts([ssd_out, swa_out, glb_out], w_out[l].astype(bf),
                           res=xs, gate=ml[2], nctx=nctx, name="out_proj")

        h2 = _normmod(xs, norm2_g[l], ml[3], ml[4], nctx)
        q = _matmul(h2, peer_wq[l].astype(bf), bf, name="peer_query")
        keys_pad, kexp = _peer_key_tables(peer_keys[l])
        route = _peer_route(q, keys_pad, n_keys)
        route = jnp.transpose(route[:, 0:2], (2, 1, 0)).reshape(nctx + s, 2 * PEER_HEADS)
        wts = _peer_weights(h2, peer_u[l].astype(bf), q, kexp, route)
        xs = _matmul(wts, peer_v[l].astype(bf), F32, res=xs, gate=ml[5], nctx=nctx, name="peer_out")

    return _finalnorm(xs, final_g, nctx)[None]
```

```python
import functools
import math

import jax
import jax.numpy as jnp
import numpy as np
from jax import lax
from jax.experimental import pallas as pl
from jax.experimental.pallas import tpu as pltpu

F32 = jnp.float32
MXU_DTYPE = jnp.bfloat16

HEAD_DIM = 128
GRID_W = 64
ROPE_THETA = 10000.0
NORM_EPS = 1e-6

SSD_HEAD_DIM = 64
SSD_GROUPS = 8
SSD_STATE = 128
SSD_CHUNK = 128
CONV_K = 5
CONV_HALO = 8

SWA_KV_HEADS = 2
GLB_KV_HEADS = 2
ATT_BLOCK = 128

PEER_HEADS = 8
PEER_HALF = 128
PEER_TOPK = 16
ROUTE_HEADS_PER_STEP = 4

NEG_BIG = -1e30
LOG2_E = 1.4426950408889634
VMEM_LIMIT = 56 * 1024 * 1024


def _pick(n, candidates):
    for c in candidates:
        if n % c == 0:
            return c
    raise ValueError(f"no tile in {candidates} divides {n}")


def _params(*sem):
    return pltpu.CompilerParams(dimension_semantics=sem, vmem_limit_bytes=VMEM_LIMIT)


def _row_select(ref, row0, tm, nctx):
    rows = row0 + lax.broadcasted_iota(jnp.int32, (tm, 1), 0)
    return jnp.where(rows < nctx, ref[0:1, :], ref[1:2, :])


def _mods_kernel(c_ref, w_ref, b_ref, o_ref):
    cv = c_ref[...]
    sv = cv * (1.0 / (1.0 + jnp.exp(-cv)))
    acc = jnp.dot(sv.astype(MXU_DTYPE), w_ref[...].astype(MXU_DTYPE),
                  preferred_element_type=F32)
    o_ref[...] = acc + b_ref[...]


def _mods(cvec, ada_w, ada_b):
    depth, d, n = ada_w.shape
    tn = _pick(n, (512, 256, 128))
    return pl.pallas_call(
        _mods_kernel,
        grid=(depth, n // tn),
        in_specs=[
            pl.BlockSpec((8, d), lambda l, j: (0, 0)),
            pl.BlockSpec((None, d, tn), lambda l, j: (l, 0, j)),
            pl.BlockSpec((None, 1, tn), lambda l, j: (l, 0, j)),
        ],
        out_specs=pl.BlockSpec((None, 8, tn), lambda l, j: (l, 0, j)),
        out_shape=jax.ShapeDtypeStruct((depth, 8, n), F32),
        compiler_params=_params("parallel", "parallel"),
        name="adaln_mods",
    )(cvec, ada_w, ada_b.reshape(depth, 1, n))


NORM_ROWS = 16
NORM_UNROLL = 8


def _normmod_kernel(x_ref, g_ref, shift_ref, scale_ref, o_ref, gain_ref, bias_ref, *, tm, nctx):
    cls = jnp.where(pl.program_id(0) * tm < nctx, 0, 1)
    d = x_ref.shape[1]
    gain = g_ref[...] * (1.0 + scale_ref[pl.ds(cls, 1), :])
    gain_ref[...] = jnp.broadcast_to(gain, (NORM_ROWS, d))
    bias_ref[...] = jnp.broadcast_to(shift_ref[pl.ds(cls, 1), :], (NORM_ROWS, d))

    def body(r, carry):
        rows = pl.ds(pl.multiple_of(r * NORM_ROWS, NORM_ROWS), NORM_ROWS)
        x = x_ref[rows, :]
        inv = lax.rsqrt(jnp.mean(x * x, axis=-1, keepdims=True) + NORM_EPS)
        y = x_ref[rows, :] * inv * gain_ref[...] + bias_ref[...]
        o_ref[rows, :] = y.astype(o_ref.dtype)
        return carry

    lax.fori_loop(0, tm // NORM_ROWS, body, 0, unroll=NORM_UNROLL)


def _normmod(xs, g, shift, scale, nctx):
    m, d = xs.shape
    tm = _pick(math.gcd(m, nctx), (256, 128))
    return pl.pallas_call(
        functools.partial(_normmod_kernel, tm=tm, nctx=nctx),
        grid=(m // tm,),
        in_specs=[
            pl.BlockSpec((tm, d), lambda i: (i, 0)),
            pl.BlockSpec((1, d), lambda i: (0, 0)),
            pl.BlockSpec((2, d), lambda i: (0, 0)),
            pl.BlockSpec((2, d), lambda i: (0, 0)),
        ],
        out_specs=pl.BlockSpec((tm, d), lambda i: (i, 0)),
        out_shape=jax.ShapeDtypeStruct((m, d), MXU_DTYPE),
        scratch_shapes=[pltpu.VMEM((NORM_ROWS, d), F32), pltpu.VMEM((NORM_ROWS, d), F32)],
        compiler_params=_params("parallel"),
        name="norm_modulate",
    )(xs, g.reshape(1, d), shift, scale)


def _finalnorm_kernel(x_ref, g_ref, o_ref):
    def body(r, carry):
        rows = pl.ds(pl.multiple_of(r * NORM_ROWS, NORM_ROWS), NORM_ROWS)
        x = x_ref[rows, :]
        inv = lax.rsqrt(jnp.mean(x * x, axis=-1, keepdims=True) + NORM_EPS)
        o_ref[rows, :] = x_ref[rows, :] * inv * g_ref[...]
        return carry

    lax.fori_loop(0, x_ref.shape[0] // NORM_ROWS, body, 0, unroll=NORM_UNROLL)


def _finalnorm(xs, g, nctx):
    m, d = xs.shape
    s = m - nctx
    tm = _pick(math.gcd(s, nctx), (256, 128))
    off = nctx // tm
    return pl.pallas_call(
        _finalnorm_kernel,
        grid=(s // tm,),
        in_specs=[
            pl.BlockSpec((tm, d), lambda i: (i + off, 0)),
            pl.BlockSpec((1, d), lambda i: (0, 0)),
        ],
        out_specs=pl.BlockSpec((tm, d), lambda i: (i, 0)),
        out_shape=jax.ShapeDtypeStruct((s, d), F32),
        compiler_params=_params("parallel"),
        name="final_norm",
    )(xs, g.reshape(1, d))


def _mm_kernel(a_ref, b_ref, o_ref):
    o_ref[...] = jnp.dot(a_ref[...], b_ref[...],
                         preferred_element_type=F32).astype(o_ref.dtype)


def _mm_res_kernel(a_ref, b_ref, res_ref, gate_ref, o_ref, *, tm, nctx):
    acc = jnp.dot(a_ref[...], b_ref[...], preferred_element_type=F32)
    gate = _row_select(gate_ref, pl.program_id(1) * tm, tm, nctx)
    o_ref[...] = res_ref[...] + gate * acc


def _matmul(a, b, out_dtype, *, layer=None, res=None, gate=None, nctx=0, name="matmul"):
    m, k = a.shape
    n = b.shape[-1]
    tm = _pick(m, (640, 256, 128))
    tn = _pick(n, (512, 640, 256, 128))
    a_spec = pl.BlockSpec((tm, k), lambda j, i: (i, 0))
    if layer is None:
        b_spec = pl.BlockSpec((k, tn), lambda j, i: (0, j))
    else:
        b_spec = pl.BlockSpec((None, k, tn), lambda j, i: (layer, 0, j))
    o_spec = pl.BlockSpec((tm, tn), lambda j, i: (i, j))
    if res is None:
        kern, specs, args = _mm_kernel, [a_spec, b_spec], (a, b)
    else:
        kern = functools.partial(_mm_res_kernel, tm=tm, nctx=nctx)
        specs = [a_spec, b_spec, o_spec, pl.BlockSpec((2, tn), lambda j, i: (0, j))]
        args = (a, b, res, gate)
    return pl.pallas_call(
        kern,
        grid=(n // tn, m // tm),
        in_specs=specs,
        out_specs=o_spec,
        out_shape=jax.ShapeDtypeStruct((m, n), out_dtype),
        compiler_params=_params("parallel", "parallel"),
        name=name,
    )(*args)


def _mm_parts_res_kernel(*refs, n_parts, tm, nctx):
    a_refs, b_refs = refs[:n_parts], refs[n_parts:2 * n_parts]
    res_ref, gate_ref, o_ref = refs[2 * n_parts:]
    acc = jnp.dot(a_refs[0][...], b_refs[0][...], preferred_element_type=F32)
    for a_ref, b_ref in zip(a_refs[1:], b_refs[1:]):
        acc = acc + jnp.dot(a_ref[...], b_ref[...], preferred_element_type=F32)
    gate = _row_select(gate_ref, pl.program_id(1) * tm, tm, nctx)
    o_ref[...] = res_ref[...] + gate * acc


def _matmul_parts(parts, b_all, layer, *, res, gate, nctx, name):
    m = parts[0].shape[0]
    n = b_all.shape[2]
    tm = _pick(m, (640, 256, 128))
    tn = _pick(n, (512, 256, 128))
    a_specs, b_specs, off = [], [], 0
    for p in parts:
        kp = p.shape[1]
        assert off % kp == 0, "each part's rows of b must start on a multiple of its width"
        a_specs.append(pl.BlockSpec((tm, kp), lambda j, i: (i, 0)))
        b_specs.append(pl.BlockSpec((None, kp, tn), functools.partial(
            lambda j, i, blk: (layer, blk, j), blk=off // kp)))
        off += kp
    assert off == b_all.shape[1]
    b = b_all
    o_spec = pl.BlockSpec((tm, tn), lambda j, i: (i, j))
    return pl.pallas_call(
        functools.partial(_mm_parts_res_kernel, n_parts=len(parts), tm=tm, nctx=nctx),
        grid=(n // tn, m // tm),
        in_specs=a_specs + b_specs + [o_spec, pl.BlockSpec((2, tn), lambda j, i: (0, j))],
        out_specs=o_spec,
        out_shape=jax.ShapeDtypeStruct((m, n), F32),
        compiler_params=_params("parallel", "parallel"),
        name=name,
    )(*parts, *([b] * len(parts)), res, gate)


def _ssd_chunk_of_step(s, ncc, nch, reverse):
    if not reverse:
        return s
    return jnp.where(s < ncc, ncc - 1 - s, nch - 1 + ncc - s)


def _ssd_conv_kernel(xbc_ref, prev_ref, next_ref, cw_ref, cb_ref, o_ref, xw_ref, *, ncc, nch):
    q = SSD_CHUNK
    c = pl.program_id(0)
    has_prev = jnp.logical_and(c != 0, c != ncc)
    has_next = jnp.logical_and(c != ncc - 1, c != nch - 1)
    xw_ref[0:CONV_HALO, :] = jnp.where(has_prev, prev_ref[...], 0.0)
    xw_ref[CONV_HALO:CONV_HALO + q, :] = xbc_ref[...]
    xw_ref[CONV_HALO + q:, :] = jnp.where(has_next, next_ref[...], 0.0)
    u = jnp.zeros(o_ref.shape, F32) + cb_ref[...]
    for kk in range(CONV_K):
        start = CONV_HALO - CONV_K // 2 + kk
        u = u + xw_ref[start:start + q, :] * cw_ref[kk:kk + 1, :]
    o_ref[...] = u * (1.0 / (1.0 + jnp.exp(-u)))


def _ssd_conv(zx, conv_w8, conv_b, nctx):
    m = zx.shape[0]
    q = SSD_CHUNK
    nch, ncc = m // q, nctx // q
    n_conv = conv_b.shape[1]
    hb = q // CONV_HALO
    n_hb = m // CONV_HALO
    return pl.pallas_call(
        functools.partial(_ssd_conv_kernel, ncc=ncc, nch=nch),
        grid=(nch,),
        in_specs=[
            pl.BlockSpec((q, n_conv), lambda c: (c, 0)),
            pl.BlockSpec((CONV_HALO, n_conv), lambda c: (jnp.maximum(c * hb - 1, 0), 0)),
            pl.BlockSpec((CONV_HALO, n_conv), lambda c: (jnp.minimum((c + 1) * hb, n_hb - 1), 0)),
            pl.BlockSpec((8, n_conv), lambda c: (0, 0)),
            pl.BlockSpec((1, n_conv), lambda c: (0, 0)),
        ],
        out_specs=pl.BlockSpec((q, n_conv), lambda c: (c, 0)),
        out_shape=jax.ShapeDtypeStruct((m, n_conv), F32),
        scratch_shapes=[pltpu.VMEM((q + 2 * CONV_HALO, n_conv), F32)],
        compiler_params=_params("parallel"),
        name="ssd_conv",
    )(zx, zx, zx, conv_w8, conv_b)


def _ssd_kernel(*refs, reverse, ncc, nch):
    if reverse:
        (u_ref, dt_ref, dtb_ref, a_ref, z_ref, yf_ref, dsk_ref, ng_ref, o_ref, h_ref) = refs
    else:
        (u_ref, dt_ref, dtb_ref, a_ref, o_ref, h_ref) = refs
    q = SSD_CHUNK
    step = pl.program_id(0)
    n_xs = SSD_GROUPS * 4 * SSD_HEAD_DIM
    n_bc = SSD_GROUPS * SSD_STATE
    n_heads = n_xs // SSD_HEAD_DIM

    @pl.when(step == 0)
    def _():
        h_ref[...] = jnp.zeros_like(h_ref)

    u = u_ref[...]

    dtr = dt_ref[...] + dtb_ref[...]
    dt = jnp.maximum(dtr, 0.0) + jnp.log1p(jnp.exp(-jnp.abs(dtr)))
    d_a = dt * a_ref[...]
    ii = lax.broadcasted_iota(jnp.int32, (q, q), 0)
    jj = lax.broadcasted_iota(jnp.int32, (q, q), 1)
    mask = (jj >= ii) if reverse else (jj <= ii)
    tri = jnp.where(mask, 1.0, 0.0).astype(F32)
    cs = jnp.dot(tri, d_a, preferred_element_type=F32, precision=lax.Precision.HIGHEST)
    cs_t = cs.T
    dt_t = dt.T
    last = 0 if reverse else q - 1
    total = cs[last:last + 1, :]
    w_state_t = (dt * jnp.exp(total - cs)).T
    e_cs = jnp.exp(cs)
    chunk_decay = jnp.exp(total)
    lane = lax.broadcasted_iota(jnp.int32, (1, 2 * SSD_HEAD_DIM), 1)
    first_half = lane < SSD_HEAD_DIM
    dir_off = n_heads if reverse else 0

    y_parts = []
    for g in range(SSD_GROUPS):
        bg = u[:, n_xs + g * SSD_STATE:n_xs + (g + 1) * SSD_STATE]
        cg = u[:, n_xs + n_bc + g * SSD_STATE:n_xs + n_bc + (g + 1) * SSD_STATE]
        cb = lax.dot_general(cg.astype(MXU_DTYPE), bg.astype(MXU_DTYPE),
                             (((1,), (1,)), ((), ())), preferred_element_type=F32)
        bg_t = bg.T
        for pair in range(2):
            lo = g * 4 * SSD_HEAD_DIM + pair * 2 * SSD_HEAD_DIM
            xs_pair = u[:, lo:lo + 2 * SSD_HEAD_DIM].astype(MXU_DTYPE)
            h_pair = h_ref[g, :, pair * 128:(pair + 1) * 128]
            rhs = jnp.concatenate([xs_pair, h_pair.astype(MXU_DTYPE)], axis=0)
            ys, sts, decs = [], [], []
            for r in range(2):
                col = dir_off + g * 4 + pair * 2 + r
                seg = jnp.exp(jnp.where(mask, cs[:, col:col + 1] - cs_t[col:col + 1, :], -jnp.inf))
                m_h = cb * seg * dt_t[col:col + 1, :]
                c_h = cg * e_cs[:, col:col + 1]
                lhs = jnp.concatenate([m_h.astype(MXU_DTYPE), c_h.astype(MXU_DTYPE)], axis=1)
                ys.append(jnp.dot(lhs, rhs, preferred_element_type=F32))
                b_h = bg_t * w_state_t[col:col + 1, :]
                sts.append(jnp.dot(b_h.astype(MXU_DTYPE), xs_pair, preferred_element_type=F32))
                decs.append(chunk_decay[:, col:col + 1])
            y_parts.append(jnp.where(first_half, ys[0], ys[1]))
            dec = jnp.where(first_half, decs[0], decs[1])
            h_ref[g, :, pair * 128:(pair + 1) * 128] = (
                h_pair * dec + jnp.where(first_half, sts[0], sts[1]))
    y = jnp.concatenate(y_parts, axis=1)

    if not reverse:
        o_ref[...] = y
    else:
        y = y + yf_ref[...] + dsk_ref[...] * u[:, :n_xs]
        zz = z_ref[...]
        y = y * (zz * (1.0 / (1.0 + jnp.exp(-zz))))
        gw = n_xs // SSD_GROUPS
        outs = []
        for g in range(SSD_GROUPS):
            sl = y[:, g * gw:(g + 1) * gw]
            outs.append(sl * lax.rsqrt(jnp.mean(sl * sl, axis=-1, keepdims=True) + NORM_EPS))
        o_ref[...] = (jnp.concatenate(outs, axis=1) * ng_ref[...]).astype(o_ref.dtype)


def _ssd(u, zx, p2, dtb, a_vec, dsk, ng, nctx, *, reverse, yf=None):
    m = zx.shape[0]
    q = SSD_CHUNK
    nch, ncc = m // q, nctx // q
    n_xs = SSD_GROUPS * 4 * SSD_HEAD_DIM
    n_conv = u.shape[1]
    dt_blk = (p2.shape[1] - 128) // 128

    def cidx(s):
        return _ssd_chunk_of_step(s, ncc, nch, reverse)

    in_specs = [
        pl.BlockSpec((q, n_conv), lambda s: (cidx(s), 0)),
        pl.BlockSpec((q, 128), lambda s: (cidx(s), dt_blk)),
        pl.BlockSpec((1, 128), lambda s: (0, 0)),
        pl.BlockSpec((1, 128), lambda s: (0, 0)),
    ]
    args = [u, p2, dtb, a_vec]
    if reverse:
        in_specs += [
            pl.BlockSpec((q, n_xs), lambda s: (cidx(s), n_conv // n_xs)),
            pl.BlockSpec((q, n_xs), lambda s: (cidx(s), 0)),
            pl.BlockSpec((1, n_xs), lambda s: (0, 0)),
            pl.BlockSpec((1, n_xs), lambda s: (0, 0)),
        ]
        args += [zx, yf, dsk, ng]
    return pl.pallas_call(
        functools.partial(_ssd_kernel, reverse=reverse, ncc=ncc, nch=nch),
        grid=(nch,),
        in_specs=in_specs,
        out_specs=pl.BlockSpec((q, n_xs), lambda s: (cidx(s), 0)),
        out_shape=jax.ShapeDtypeStruct((m, n_xs), MXU_DTYPE if reverse else F32),
        scratch_shapes=[pltpu.VMEM((SSD_GROUPS, SSD_STATE, 4 * SSD_HEAD_DIM), F32)],
        compiler_params=_params("arbitrary"),
        name="ssd_bwd_finish" if reverse else "ssd_fwd",
    )(*args)


_QKV_SLOTS = (
    (8, None, True, True),
    (2, None, True, False),
    (2, None, False, False),
    (8, "q", True, True),
    (2, "k", True, False),
    (2, None, False, False),
)
N_QKV = sum(s[0] for s in _QKV_SLOTS) * HEAD_DIM


GLB_V_SLOT = 22
SWA_V_SLOT = 10
VT_ROWS = HEAD_DIM + 16
KV_CHUNK = 256


def _qkv_kernel(p_ref, cos_ref, sin_ref, qg_ref, kg_ref, o_ref, vt_ref, vts_ref):
    tm = p_ref.shape[0]
    row = lax.broadcasted_iota(jnp.int32, (VT_ROWS - HEAD_DIM, tm), 0)
    ones_rows = jnp.where(row == 0, 1.0, 0.0).astype(vt_ref.dtype)
    for h in range(GLB_KV_HEADS):
        v = p_ref[:, (GLB_V_SLOT + h) * HEAD_DIM:(GLB_V_SLOT + h + 1) * HEAD_DIM]
        vt_ref[h, 0, 0:HEAD_DIM, :] = v.T.astype(vt_ref.dtype)
        vt_ref[h, 0, HEAD_DIM:VT_ROWS, :] = ones_rows
    row_blk = lax.broadcasted_iota(jnp.int32, (VT_ROWS - HEAD_DIM, ATT_BLOCK), 0)
    ones_blk = jnp.where(row_blk == 0, 1.0, 0.0).astype(vts_ref.dtype)
    for h in range(SWA_KV_HEADS):
        for b in range(tm // ATT_BLOCK):
            v = p_ref[b * ATT_BLOCK:(b + 1) * ATT_BLOCK,
                      (SWA_V_SLOT + h) * HEAD_DIM:(SWA_V_SLOT + h + 1) * HEAD_DIM]
            vts_ref[h, b, 0:HEAD_DIM, :] = v.T.astype(vts_ref.dtype)
            vts_ref[h, b, HEAD_DIM:VT_ROWS, :] = ones_blk
    cos = cos_ref[...]
    sin = sin_ref[...]
    lane = lax.broadcasted_iota(jnp.int32, (1, HEAD_DIM), 1)
    lower = (lane % (HEAD_DIM // 2)) < (HEAD_DIM // 4)
    slot = 0
    for count, norm, rope, scale in _QKV_SLOTS:
        for _ in range(count):
            t = p_ref[:, slot * HEAD_DIM:(slot + 1) * HEAD_DIM]
            if norm is not None:
                g = qg_ref[...] if norm == "q" else kg_ref[...]
                t = t * lax.rsqrt(jnp.mean(t * t, axis=-1, keepdims=True) + NORM_EPS) * g
            if rope:
                swapped = jnp.where(lower, pltpu.roll(t, HEAD_DIM - HEAD_DIM // 4, 1),
                                    pltpu.roll(t, HEAD_DIM // 4, 1))
                t = t * cos + swapped * sin
            if scale:
                t = t * (HEAD_DIM ** -0.5 * LOG2_E)
            o_ref[:, slot * HEAD_DIM:(slot + 1) * HEAD_DIM] = t.astype(o_ref.dtype)
            slot += 1


def _qkv_prep(p2, cos, sin, qg, kg):
    m = p2.shape[0]
    tm = KV_CHUNK
    return pl.pallas_call(
        _qkv_kernel,
        grid=(m // tm,),
        in_specs=[
            pl.BlockSpec((tm, N_QKV), lambda i: (i, 0)),
            pl.BlockSpec((tm, HEAD_DIM), lambda i: (i, 0)),
            pl.BlockSpec((tm, HEAD_DIM), lambda i: (i, 0)),
            pl.BlockSpec((1, HEAD_DIM), lambda i: (0, 0)),
            pl.BlockSpec((1, HEAD_DIM), lambda i: (0, 0)),
        ],
        out_specs=[
            pl.BlockSpec((tm, N_QKV), lambda i: (i, 0)),
            pl.BlockSpec((GLB_KV_HEADS, 1, VT_ROWS, tm), lambda i: (0, i, 0, 0)),
            pl.BlockSpec((SWA_KV_HEADS, tm // ATT_BLOCK, VT_ROWS, ATT_BLOCK), lambda i: (0, i, 0, 0)),
        ],
        out_shape=[
            jax.ShapeDtypeStruct((m, N_QKV), MXU_DTYPE),
            jax.ShapeDtypeStruct((GLB_KV_HEADS, m // tm, VT_ROWS, tm), MXU_DTYPE),
            jax.ShapeDtypeStruct((SWA_KV_HEADS, m // ATT_BLOCK, VT_ROWS, ATT_BLOCK), MXU_DTYPE),
        ],
        compiler_params=_params("parallel"),
        name="qkv_prep",
    )(p2, cos, sin, qg.reshape(1, HEAD_DIM), kg.reshape(1, HEAD_DIM))


def _rope_tables(s, nctx):
    t = jnp.arange(s, dtype=jnp.int32)
    pos_r = (t // GRID_W).astype(F32)
    pos_c = (t % GRID_W).astype(F32)
    n_freq = HEAD_DIM // 4
    inv = ROPE_THETA ** (-jnp.arange(n_freq, dtype=F32) / n_freq)
    ar, ac = pos_r[:, None] * inv, pos_c[:, None] * inv
    cos = jnp.concatenate([jnp.cos(ar), jnp.cos(ar), jnp.cos(ac), jnp.cos(ac)], axis=1)
    sin = jnp.concatenate([-jnp.sin(ar), jnp.sin(ar), -jnp.sin(ac), jnp.sin(ac)], axis=1)
    cos = jnp.concatenate([jnp.ones((nctx, HEAD_DIM), F32), cos], axis=0)
    sin = jnp.concatenate([jnp.zeros((nctx, HEAD_DIM), F32), sin], axis=0)
    return cos, sin


def _stack_heads(q, n):
    return jnp.concatenate([q[:, g * HEAD_DIM:(g + 1) * HEAD_DIM] for g in range(n)], axis=0)


def _unstack_heads(o, n, rows):
    return jnp.concatenate([o[g * rows:(g + 1) * rows, :] for g in range(n)], axis=1)


_NT = (((1,), (1,)), ((), ()))


def _glb_kernel(q_ref, k_ref, vt_ref, o_ref, acc_ref, *, tq, nk, unroll, gq):
    tk = KV_CHUNK
    acc_ref[...] = jnp.zeros_like(acc_ref)
    qs = [q_ref[:, g * HEAD_DIM:(g + 1) * HEAD_DIM] for g in range(gq)]

    def body(jj, ms):
        ms = list(ms)

        def scores(c):
            start = pl.multiple_of((jj * unroll + c) * tk, tk)
            kb = k_ref[pl.ds(start, tk), :]
            return [lax.dot_general(kb, q, _NT, preferred_element_type=F32) for q in qs]

        def softmax(s_ts):
            alphas, p_ts = [], []
            for g in range(gq):
                m_new = jnp.maximum(ms[g], jnp.max(s_ts[g], axis=0, keepdims=True))
                alphas.append(jnp.exp2(ms[g] - m_new))
                p_ts.append(jnp.exp2(s_ts[g] - m_new).astype(MXU_DTYPE))
                ms[g] = m_new
            return alphas, p_ts

        def accumulate(c, alphas, p_ts):
            vtb = vt_ref[jj * unroll + c]
            pvs = [jnp.dot(vtb, p_t, preferred_element_type=F32) for p_t in p_ts]
            for g in range(gq):
                acc_ref[g] = alphas[g] * acc_ref[g] + pvs[g]

        s_cur = scores(0)
        s_next = scores(1) if unroll > 1 else None
        prob = softmax(s_cur)
        for c in range(1, unroll):
            s_cur = s_next
            s_next = scores(c + 1) if c + 1 < unroll else None
            accumulate(c - 1, *prob)
            prob = softmax(s_cur)
        accumulate(unroll - 1, *prob)
        return tuple(ms)

    init = tuple(jnp.full((1, tq), NEG_BIG, F32) for _ in range(gq))
    lax.fori_loop(0, nk // unroll, body, init)
    for g in range(gq):
        acc = acc_ref[g]
        out_t = acc[0:HEAD_DIM, :] / acc[HEAD_DIM:HEAD_DIM + 1, :]
        o_ref[:, g * HEAD_DIM:(g + 1) * HEAD_DIM] = out_t.T.astype(o_ref.dtype)


def _glb_attention(qkv, vt, nctx):
    m = qkv.shape[0]
    s = m - nctx
    gq = 8 // GLB_KV_HEADS
    tq = KV_CHUNK
    qw = gq * HEAD_DIM
    q_blk0 = (12 * HEAD_DIM) // qw
    nk = m // KV_CHUNK
    unroll = _pick(nk, (13, 5, 3, 2, 1))
    return pl.pallas_call(
        functools.partial(_glb_kernel, tq=tq, nk=nk, unroll=unroll, gq=gq),
        grid=(GLB_KV_HEADS, s // tq),
        in_specs=[
            pl.BlockSpec((tq, qw), lambda h, i: (i + nctx // tq, q_blk0 + h)),
            pl.BlockSpec((m, HEAD_DIM), lambda h, i: (0, 20 + h)),
            pl.BlockSpec((None, nk, VT_ROWS, KV_CHUNK), lambda h, i: (h, 0, 0, 0)),
        ],
        out_specs=pl.BlockSpec((tq, qw), lambda h, i: (i + nctx // tq, h)),
        out_shape=jax.ShapeDtypeStruct((m, GLB_KV_HEADS * qw), MXU_DTYPE),
        scratch_shapes=[pltpu.VMEM((gq, VT_ROWS, tq), F32)],
        compiler_params=_params("parallel", "parallel"),
        name="global_attention",
    )(qkv, qkv, vt)


def _swa_kernel(q_ref, k_ref, vt_ref, sink_ref, o_ref, *, nctx, nb, gq):
    blk = ATT_BLOCK
    n = pl.program_id(1)
    base = nctx + n * blk
    starts = (pl.multiple_of(base - blk, blk), pl.multiple_of(base, blk),
              pl.multiple_of(jnp.minimum(base + blk, nctx + (nb - 1) * blk), blk))
    kj = lax.broadcasted_iota(jnp.int32, (blk, 2 * blk), 0)
    qi = lax.broadcasted_iota(jnp.int32, (blk, 2 * blk), 1) % blk
    k_parts = [k_ref[pl.ds(st, blk), :] for st in starts] + [k_ref[0:nctx, :]]
    vt = jnp.concatenate([vt_ref[st // blk] for st in starts]
                         + [vt_ref[c] for c in range(nctx // blk)], axis=1)
    pairs = range(gq // 2)
    q2s = [jnp.concatenate([q_ref[:, (2 * pr + r) * HEAD_DIM:(2 * pr + r + 1) * HEAD_DIM]
                            for r in range(2)], axis=0) for pr in pairs]
    scores = [[lax.dot_general(kp, q2, _NT, preferred_element_type=F32) for kp in k_parts]
              for q2 in q2s]
    probs, maxes = [], []
    for pr in pairs:
        s = scores[pr]
        s[0] = jnp.where(jnp.logical_and(kj >= qi, n > 0), s[0], NEG_BIG)
        s[2] = jnp.where(jnp.logical_and(kj <= qi, n < nb - 1), s[2], NEG_BIG)
        m_i = sink_ref[pr]
        for t in s:
            m_i = jnp.maximum(m_i, jnp.max(t, axis=0, keepdims=True))
        probs.append(jnp.concatenate([jnp.exp2(t - m_i).astype(MXU_DTYPE) for t in s], axis=0))
        maxes.append(m_i)
    accs = [jnp.dot(vt, p, preferred_element_type=F32) for p in probs]
    for pr in pairs:
        acc = accs[pr]
        l_i = acc[HEAD_DIM:HEAD_DIM + 1, :] + jnp.exp2(sink_ref[pr] - maxes[pr])
        out = (acc[0:HEAD_DIM, :] / l_i).T
        for r in range(2):
            lo = (2 * pr + r) * HEAD_DIM
            o_ref[:, lo:lo + HEAD_DIM] = out[r * blk:(r + 1) * blk, :].astype(o_ref.dtype)


def _sink_rows(sink, kv_heads, rows_per_head):
    gq = sink.shape[0] // kv_heads
    t = jnp.repeat(sink.reshape(kv_heads, gq).astype(F32) * LOG2_E, rows_per_head, axis=1)
    return jnp.broadcast_to(t[:, :, None], (kv_heads, gq * rows_per_head, 128))


def _swa_attention(qkv, vt, sink, nctx):
    m = qkv.shape[0]
    s = m - nctx
    blk = ATT_BLOCK
    gq = 8 // SWA_KV_HEADS
    nb = s // blk
    qw = gq * HEAD_DIM
    sink_rows = jnp.repeat(sink.astype(F32) * LOG2_E, blk).reshape(SWA_KV_HEADS, gq // 2, 1, 2 * blk)
    return pl.pallas_call(
        functools.partial(_swa_kernel, nctx=nctx, nb=nb, gq=gq),
        grid=(SWA_KV_HEADS, nb),
        in_specs=[
            pl.BlockSpec((blk, qw), lambda h, i: (i + nctx // blk, h)),
            pl.BlockSpec((m, HEAD_DIM), lambda h, i: (0, 8 + h)),
            pl.BlockSpec((None, m // blk, VT_ROWS, blk), lambda h, i: (h, 0, 0, 0)),
            pl.BlockSpec((None, gq // 2, 1, 2 * blk), lambda h, i: (h, 0, 0, 0)),
        ],
        out_specs=pl.BlockSpec((blk, qw), lambda h, i: (i + nctx // blk, h)),
        out_shape=jax.ShapeDtypeStruct((m, SWA_KV_HEADS * qw), MXU_DTYPE),
        compiler_params=_params("parallel", "parallel"),
        name="window_attention",
    )(qkv, qkv, vt, sink_rows)


def _ctx_kernel(*refs, gq, nctx, has_sink):
    if has_sink:
        q_ref, k_ref, v_ref, sink_ref, _, o_ref = refs
    else:
        q_ref, k_ref, v_ref, _, o_ref = refs
    qs = _stack_heads(q_ref[...], gq)
    s = lax.dot_general(qs, k_ref[...], _NT, preferred_element_type=F32)
    m_i = jnp.max(s, axis=-1, keepdims=True)
    if has_sink:
        sink = sink_ref[:, 0:1]
        m_i = jnp.maximum(m_i, sink)
    p = jnp.exp2(s - m_i)
    l_i = jnp.sum(p, axis=-1, keepdims=True)
    if has_sink:
        l_i = l_i + jnp.exp2(sink - m_i)
    acc = jnp.dot(p.astype(MXU_DTYPE), v_ref[...], preferred_element_type=F32)
    o_ref[...] = _unstack_heads(acc / l_i, gq, nctx).astype(o_ref.dtype)


def _ctx_attention(qkv, dst, nctx, q_slot, k_slot, v_slot, sink):
    gq = 4
    qw = gq * HEAD_DIM
    has_sink = sink is not None
    in_specs = [
        pl.BlockSpec((nctx, qw), lambda h: (0, q_slot // gq + h)),
        pl.BlockSpec((nctx, HEAD_DIM), lambda h: (0, k_slot + h)),
        pl.BlockSpec((nctx, HEAD_DIM), lambda h: (0, v_slot + h)),
    ]
    args = [qkv, qkv, qkv]
    if has_sink:
        in_specs.append(pl.BlockSpec((None, gq * nctx, 128), lambda h: (h, 0, 0)))
        args.append(_sink_rows(sink, 2, nctx))
    in_specs.append(pl.BlockSpec(memory_space=pl.ANY))
    args.append(dst)
    return pl.pallas_call(
        functools.partial(_ctx_kernel, gq=gq, nctx=nctx, has_sink=has_sink),
        grid=(2,),
        in_specs=in_specs,
        out_specs=pl.BlockSpec((nctx, qw), lambda h: (0, h)),
        out_shape=jax.ShapeDtypeStruct(dst.shape, dst.dtype),
        input_output_aliases={len(args) - 1: 0},
        compiler_params=_params("parallel"),
        name="context_attention",
    )(*args)


def _top_rows(s, count):
    rows = []
    for _ in range(count):
        mx = jnp.max(s, axis=0, keepdims=True)
        rows.append(mx)
        s = jnp.where(s == mx, NEG_BIG, s)
    return rows


def _route_kernel(q_ref, keys_ref, o_ref, *, n_keys, heads):
    for hh in range(heads):
        _route_one_head(q_ref, keys_ref, o_ref, hh, n_keys)


def _route_one_head(q_ref, keys_ref, o_ref, hh, n_keys):
    tops = []
    for half in range(2):
        lo_lane = (2 * hh + half) * PEER_HALF
        qh = q_ref[:, lo_lane:lo_lane + PEER_HALF]
        s_t = lax.dot_general(keys_ref[hh, half], qh, _NT, preferred_element_type=F32)
        tops.append(_top_rows(s_t[0:n_keys, :], PEER_TOPK + 1))
    a_top, b_top = tops
    a_all = jnp.concatenate(a_top[:PEER_TOPK], axis=0)
    b_all = jnp.concatenate(b_top[:PEER_TOPK], axis=0)
    lo, hi = slice(0, 8), slice(8, PEER_TOPK)
    upper = lax.broadcasted_iota(jnp.int32, (8, 1), 0) >= 4
    cand = jnp.concatenate(
        [a_top[0] + b_all[lo], a_top[0] + b_all[hi], b_top[0] + a_all[hi]]
        + [a_top[i] + b_all[lo] for i in (1, 2, 3)]
        + [jnp.where(upper, b_top[j] + a_all[lo], NEG_BIG) for j in (0, 1, 2)], axis=0)
    c = _top_rows(cand, PEER_TOPK + 1)
    runner_up = jnp.maximum(c[PEER_TOPK], jnp.maximum(a_top[PEER_TOPK] + b_top[0],
                                                       a_top[0] + b_top[PEER_TOPK]))
    thr = 0.5 * (c[PEER_TOPK - 1] + runner_up)
    z = jnp.zeros_like(thr)
    for kk in range(PEER_TOPK):
        z = z + jnp.exp(c[kk] - c[0])
    pad = jnp.zeros((6, thr.shape[1]), F32)
    o_ref[hh] = jnp.concatenate([thr, c[0] + jnp.log(z), pad], axis=0)


def _peer_route(q, keys_pad, n_keys):
    m = q.shape[0]
    tm = _pick(m, (256, 128))
    heads = ROUTE_HEADS_PER_STEP
    return pl.pallas_call(
        functools.partial(_route_kernel, n_keys=n_keys, heads=heads),
        grid=(m // tm, PEER_HEADS // heads),
        in_specs=[
            pl.BlockSpec((tm, heads * 2 * PEER_HALF), lambda i, h: (i, h)),
            pl.BlockSpec((heads, 2, 128, PEER_HALF), lambda i, h: (h, 0, 0, 0)),
        ],
        out_specs=pl.BlockSpec((heads, 8, tm), lambda i, h: (h, 0, i)),
        out_shape=jax.ShapeDtypeStruct((PEER_HEADS, 8, m), F32),
        compiler_params=_params("parallel", "parallel"),
        name="peer_route",
    )(q, keys_pad)


def _peer_w_kernel(h_ref, u_ref, q_ref, kx_ref, r_ref, o_ref, *, slab):
    kd = 2 * PEER_HALF
    for c in range(o_ref.shape[1] // slab):
        sl = slice(c * slab, (c + 1) * slab)
        a = lax.dot_general(h_ref[...], u_ref[sl, :], _NT, preferred_element_type=F32)
        act = 0.5 * a * (1.0 + lax.erf(a * (2.0 ** -0.5)))
        g = None
        for h in range(PEER_HEADS):
            s = jnp.dot(q_ref[:, h * kd:(h + 1) * kd], kx_ref[h, :, sl],
                        preferred_element_type=F32)
            thr = r_ref[:, h:h + 1]
            log_norm = r_ref[:, PEER_HEADS + h:PEER_HEADS + h + 1]
            t = jnp.where(s >= thr, jnp.exp(s - log_norm), 0.0)
            g = t if g is None else g + t
        o_ref[:, sl] = (g * act).astype(o_ref.dtype)


def _peer_weights(h2, u_all, layer, q, kexp, route):
    m, d = h2.shape
    n_exp = u_all.shape[1]
    tm = _pick(m, (256, 128))
    slab = _pick(n_exp, (256, 128))
    te = _pick(n_exp, (5 * slab, slab))
    return pl.pallas_call(
        functools.partial(_peer_w_kernel, slab=slab),
        grid=(n_exp // te, m // tm),
        in_specs=[
            pl.BlockSpec((tm, d), lambda e, i: (i, 0)),
            pl.BlockSpec((None, te, d), lambda e, i: (layer, e, 0)),
            pl.BlockSpec((tm, q.shape[1]), lambda e, i: (i, 0)),
            pl.BlockSpec((PEER_HEADS, 2 * PEER_HALF, te), lambda e, i: (0, 0, e)),
            pl.BlockSpec((tm, route.shape[1]), lambda e, i: (i, 0)),
        ],
        out_specs=pl.BlockSpec((tm, te), lambda e, i: (i, e)),
        out_shape=jax.ShapeDtypeStruct((m, n_exp), MXU_DTYPE),
        compiler_params=_params("parallel", "parallel"),
        name="peer_expert_weights",
    )(h2, u_all, q, kexp, route)


def _peer_key_tables(keys):
    nh, _, nk, kd = keys.shape
    keys_pad = jnp.zeros((nh, 2, 128, kd), keys.dtype).at[:, :, :nk].set(keys)
    k1 = jnp.repeat(keys[:, 0], nk, axis=1)
    k2 = jnp.tile(keys[:, 1], (1, nk, 1))
    kexp = jnp.swapaxes(jnp.concatenate([k1, k2], axis=-1), 1, 2)
    return keys_pad.astype(MXU_DTYPE), kexp.astype(MXU_DTYPE)


def kernel(x, c, ctx, c_ctx, ada_w, ada_b, norm1_g, w_in, conv_w, conv_b, dt_bias, a_log, d_skip,
           ssd_norm_g, swa_sink, q_norm_g, k_norm_g, w_out, norm2_g, peer_wq, peer_keys, peer_u,
           peer_v, final_g):
    assert x.shape[0] == 1 and c.shape[0] == 1 and ctx.shape[0] == 1
    s, d = x.shape[1], x.shape[2]
    nctx = ctx.shape[1]
    depth = ada_w.shape[0]
    n_heads = d_skip.shape[1]
    n_xs = n_heads * SSD_HEAD_DIM
    n_conv = conv_w.shape[2]
    n_keys = peer_keys.shape[3]
    assert n_xs == SSD_GROUPS * 4 * SSD_HEAD_DIM and n_conv == n_xs + 2 * SSD_GROUPS * SSD_STATE
    assert nctx % 256 == 0 and s % 256 == 0 and peer_keys.shape[4] == PEER_HALF
    bf = MXU_DTYPE

    cvec = jnp.zeros((8, d), F32).at[0].set(c_ctx).at[1].set(c[0])
    mods = _mods(cvec, ada_w, ada_b)[:, 0:2].reshape(depth, 2, 6, d)
    cos, sin = _rope_tables(s, nctx)
    xs = jnp.concatenate([ctx[0], x[0]], axis=0)

    o_z, o_xbc, o_dt = 0, n_xs, n_xs + n_conv
    o_att = o_dt + 2 * n_heads
    pad_dt = jnp.zeros((d, 128 - 2 * n_heads), F32)
    w_out_b, wq_b, u_b, v_b = (t.astype(bf) for t in (w_out, peer_wq, peer_u, peer_v))

    for l in range(depth):
        ml = [mods[l, :, i] for i in range(6)]
        w = w_in[l]
        w_a = jnp.concatenate([w[:, o_xbc:o_dt], w[:, o_z:o_xbc]], axis=1).astype(bf)
        w_b = jnp.concatenate([w[:, o_att:], w[:, o_dt:o_att], pad_dt], axis=1).astype(bf)

        h1 = _normmod(xs, norm1_g[l], ml[0], ml[1], nctx)
        zx = _matmul(h1, w_a, F32, name="in_proj_ssd")
        p2 = _matmul(h1, w_b, F32, name="in_proj_attn")

        conv_w8 = jnp.zeros((8, n_conv), F32).at[:CONV_K].set(conv_w[l])
        zpad = jnp.zeros((128 - 2 * n_heads,), F32)
        dtb = jnp.concatenate([dt_bias[l].reshape(-1), zpad]).reshape(1, 128)
        a_vec = jnp.concatenate([-jnp.exp(a_log[l].reshape(-1)), zpad]).reshape(1, 128)
        dsk = jnp.repeat(d_skip[l], SSD_HEAD_DIM).reshape(1, n_xs)
        ng = ssd_norm_g[l].reshape(1, n_xs)
        cb = conv_b[l].reshape(1, n_conv)
        u = _ssd_conv(zx, conv_w8, cb, nctx)
        yf = _ssd(u, zx, p2, dtb, a_vec, dsk, ng, nctx, reverse=False)
        ssd_out = _ssd(u, zx, p2, dtb, a_vec, dsk, ng, nctx, reverse=True, yf=yf)

        qkv, vt, vt_swa = _qkv_prep(p2, cos, sin, q_norm_g[l], k_norm_g[l])
        swa_out = _swa_attention(qkv, vt_swa, swa_sink[l], nctx)
        swa_out = _ctx_attention(qkv, swa_out, nctx, 0, 8, 10, swa_sink[l])
        glb_out = _glb_attention(qkv, vt, nctx)
        glb_out = _ctx_attention(qkv, glb_out, nctx, 12, 20, GLB_V_SLOT, None)
        xs = _matmul_parts([ssd_out, swa_out, glb_out], w_out_b, l,
                           res=xs, gate=ml[2], nctx=nctx, name="out_proj")

        h2 = _normmod(xs, norm2_g[l], ml[3], ml[4], nctx)
        q = _matmul(h2, wq_b, bf, layer=l, name="peer_query")
        keys_pad, kexp = _peer_key_tables(peer_keys[l])
        route = _peer_route(q, keys_pad, n_keys)
        route = jnp.transpose(route[:, 0:2], (2, 1, 0)).reshape(nctx + s, 2 * PEER_HEADS)
        wts = _peer_weights(h2, u_b, l, q, kexp, route)
        xs = _matmul(wts, v_b, F32, layer=l, res=xs, gate=ml[5], nctx=nctx, name="peer_out")

    return _finalnorm(xs, final_g, nctx)[None]
```

```python
import functools
import math

import jax
import jax.numpy as jnp
import numpy as np
from jax import lax
from jax.experimental import pallas as pl
from jax.experimental.pallas import tpu as pltpu

F32 = jnp.float32
MXU_DTYPE = jnp.bfloat16

HEAD_DIM = 128
GRID_W = 64
ROPE_THETA = 10000.0
NORM_EPS = 1e-6

SSD_HEAD_DIM = 64
SSD_GROUPS = 8
SSD_STATE = 128
SSD_CHUNK = 128
CONV_K = 5
CONV_HALO = 8

SWA_KV_HEADS = 2
GLB_KV_HEADS = 2
ATT_BLOCK = 128

PEER_HEADS = 8
PEER_HALF = 128
PEER_TOPK = 16
ROUTE_HEADS_PER_STEP = 4

NEG_BIG = -1e30
LOG2_E = 1.4426950408889634
VMEM_LIMIT = 56 * 1024 * 1024


def _pick(n, candidates):
    for c in candidates:
        if n % c == 0:
            return c
    raise ValueError(f"no tile in {candidates} divides {n}")


def _params(*sem):
    return pltpu.CompilerParams(dimension_semantics=sem, vmem_limit_bytes=VMEM_LIMIT)


def _row_select(ref, row0, tm, nctx):
    rows = row0 + lax.broadcasted_iota(jnp.int32, (tm, 1), 0)
    return jnp.where(rows < nctx, ref[0:1, :], ref[1:2, :])


def _mods_kernel(c_ref, w_ref, b_ref, o_ref):
    cv = c_ref[...]
    sv = cv * (1.0 / (1.0 + jnp.exp(-cv)))
    acc = jnp.dot(sv.astype(MXU_DTYPE), w_ref[...].astype(MXU_DTYPE),
                  preferred_element_type=F32)
    o_ref[...] = acc + b_ref[...]


def _mods(cvec, ada_w, ada_b):
    depth, d, n = ada_w.shape
    tn = _pick(n, (512, 256, 128))
    return pl.pallas_call(
        _mods_kernel,
        grid=(depth, n // tn),
        in_specs=[
            pl.BlockSpec((8, d), lambda l, j: (0, 0)),
            pl.BlockSpec((None, d, tn), lambda l, j: (l, 0, j)),
            pl.BlockSpec((None, 1, tn), lambda l, j: (l, 0, j)),
        ],
        out_specs=pl.BlockSpec((None, 8, tn), lambda l, j: (l, 0, j)),
        out_shape=jax.ShapeDtypeStruct((depth, 8, n), F32),
        compiler_params=_params("parallel", "parallel"),
        name="adaln_mods",
    )(cvec, ada_w, ada_b.reshape(depth, 1, n))


NORM_ROWS = 16
NORM_UNROLL = 8


def _normmod_kernel(x_ref, g_ref, shift_ref, scale_ref, o_ref, gain_ref, bias_ref, *, tm, nctx):
    cls = jnp.where(pl.program_id(0) * tm < nctx, 0, 1)
    d = x_ref.shape[1]
    gain = g_ref[...] * (1.0 + scale_ref[pl.ds(cls, 1), :])
    gain_ref[...] = jnp.broadcast_to(gain, (NORM_ROWS, d))
    bias_ref[...] = jnp.broadcast_to(shift_ref[pl.ds(cls, 1), :], (NORM_ROWS, d))

    def body(r, carry):
        rows = pl.ds(pl.multiple_of(r * NORM_ROWS, NORM_ROWS), NORM_ROWS)
        x = x_ref[rows, :]
        inv = lax.rsqrt(jnp.mean(x * x, axis=-1, keepdims=True) + NORM_EPS)
        y = x_ref[rows, :] * inv * gain_ref[...] + bias_ref[...]
        o_ref[rows, :] = y.astype(o_ref.dtype)
        return carry

    lax.fori_loop(0, tm // NORM_ROWS, body, 0, unroll=NORM_UNROLL)


def _normmod(xs, g, shift, scale, nctx):
    m, d = xs.shape
    tm = _pick(math.gcd(m, nctx), (256, 128))
    return pl.pallas_call(
        functools.partial(_normmod_kernel, tm=tm, nctx=nctx),
        grid=(m // tm,),
        in_specs=[
            pl.BlockSpec((tm, d), lambda i: (i, 0)),
            pl.BlockSpec((1, d), lambda i: (0, 0)),
            pl.BlockSpec((2, d), lambda i: (0, 0)),
            pl.BlockSpec((2, d), lambda i: (0, 0)),
        ],
        out_specs=pl.BlockSpec((tm, d), lambda i: (i, 0)),
        out_shape=jax.ShapeDtypeStruct((m, d), MXU_DTYPE),
        scratch_shapes=[pltpu.VMEM((NORM_ROWS, d), F32), pltpu.VMEM((NORM_ROWS, d), F32)],
        compiler_params=_params("parallel"),
        name="norm_modulate",
    )(xs, g.reshape(1, d), shift, scale)


def _finalnorm_kernel(x_ref, g_ref, o_ref):
    def body(r, carry):
        rows = pl.ds(pl.multiple_of(r * NORM_ROWS, NORM_ROWS), NORM_ROWS)
        x = x_ref[rows, :]
        inv = lax.rsqrt(jnp.mean(x * x, axis=-1, keepdims=True) + NORM_EPS)
        o_ref[rows, :] = x_ref[rows, :] * inv * g_ref[...]
        return carry

    lax.fori_loop(0, x_ref.shape[0] // NORM_ROWS, body, 0, unroll=NORM_UNROLL)


def _finalnorm(xs, g, nctx):
    m, d = xs.shape
    s = m - nctx
    tm = _pick(math.gcd(s, nctx), (256, 128))
    off = nctx // tm
    return pl.pallas_call(
        _finalnorm_kernel,
        grid=(s // tm,),
        in_specs=[
            pl.BlockSpec((tm, d), lambda i: (i + off, 0)),
            pl.BlockSpec((1, d), lambda i: (0, 0)),
        ],
        out_specs=pl.BlockSpec((tm, d), lambda i: (i, 0)),
        out_shape=jax.ShapeDtypeStruct((s, d), F32),
        compiler_params=_params("parallel"),
        name="final_norm",
    )(xs, g.reshape(1, d))


def _mm_kernel(a_ref, b_ref, o_ref):
    o_ref[...] = jnp.dot(a_ref[...], b_ref[...],
                         preferred_element_type=F32).astype(o_ref.dtype)


def _mm_cast_kernel(a_ref, b_ref, o_ref, b_scr):
    @pl.when(pl.program_id(1) == 0)
    def _():
        b_scr[...] = b_ref[...].astype(b_scr.dtype)

    o_ref[...] = jnp.dot(a_ref[...], b_scr[...], preferred_element_type=F32).astype(o_ref.dtype)


def _matmul_f32w(a, b_all, layer, out_dtype, *, n_out, col_block, name):
    m, k = a.shape
    tm = _pick(m, (640, 256, 128))
    tn = _pick(n_out, (512, 256, 128))
    return pl.pallas_call(
        _mm_cast_kernel,
        grid=(n_out // tn, m // tm),
        in_specs=[
            pl.BlockSpec((tm, k), lambda j, i: (i, 0)),
            pl.BlockSpec((None, k, tn), lambda j, i: (layer, 0, col_block(j, tn))),
        ],
        out_specs=pl.BlockSpec((tm, tn), lambda j, i: (i, j)),
        out_shape=jax.ShapeDtypeStruct((m, n_out), out_dtype),
        scratch_shapes=[pltpu.VMEM((k, tn), MXU_DTYPE)],
        compiler_params=_params("parallel", "arbitrary"),
        name=name,
    )(a, b_all)


def _mm_res_kernel(a_ref, b_ref, res_ref, gate_ref, o_ref, *, tm, nctx):
    acc = jnp.dot(a_ref[...], b_ref[...], preferred_element_type=F32)
    gate = _row_select(gate_ref, pl.program_id(1) * tm, tm, nctx)
    o_ref[...] = res_ref[...] + gate * acc


def _matmul(a, b, out_dtype, *, layer=None, res=None, gate=None, nctx=0, name="matmul"):
    m, k = a.shape
    n = b.shape[-1]
    tm = _pick(m, (640, 256, 128))
    tn = _pick(n, (512, 640, 256, 128))
    a_spec = pl.BlockSpec((tm, k), lambda j, i: (i, 0))
    if layer is None:
        b_spec = pl.BlockSpec((k, tn), lambda j, i: (0, j))
    else:
        b_spec = pl.BlockSpec((None, k, tn), lambda j, i: (layer, 0, j))
    o_spec = pl.BlockSpec((tm, tn), lambda j, i: (i, j))
    if res is None:
        kern, specs, args = _mm_kernel, [a_spec, b_spec], (a, b)
    else:
        kern = functools.partial(_mm_res_kernel, tm=tm, nctx=nctx)
        specs = [a_spec, b_spec, o_spec, pl.BlockSpec((2, tn), lambda j, i: (0, j))]
        args = (a, b, res, gate)
    return pl.pallas_call(
        kern,
        grid=(n // tn, m // tm),
        in_specs=specs,
        out_specs=o_spec,
        out_shape=jax.ShapeDtypeStruct((m, n), out_dtype),
        compiler_params=_params("parallel", "parallel"),
        name=name,
    )(*args)


def _mm_parts_res_kernel(*refs, n_parts, tm, nctx):
    a_refs, b_refs = refs[:n_parts], refs[n_parts:2 * n_parts]
    res_ref, gate_ref, o_ref = refs[2 * n_parts:]
    acc = jnp.dot(a_refs[0][...], b_refs[0][...], preferred_element_type=F32)
    for a_ref, b_ref in zip(a_refs[1:], b_refs[1:]):
        acc = acc + jnp.dot(a_ref[...], b_ref[...], preferred_element_type=F32)
    gate = _row_select(gate_ref, pl.program_id(1) * tm, tm, nctx)
    o_ref[...] = res_ref[...] + gate * acc


def _matmul_parts(parts, b_all, layer, *, res, gate, nctx, name):
    m = parts[0].shape[0]
    n = b_all.shape[2]
    tm = _pick(m, (640, 256, 128))
    tn = _pick(n, (512, 256, 128))
    a_specs, b_specs, off = [], [], 0
    for p in parts:
        kp = p.shape[1]
        assert off % kp == 0, "each part's rows of b must start on a multiple of its width"
        a_specs.append(pl.BlockSpec((tm, kp), lambda j, i: (i, 0)))
        b_specs.append(pl.BlockSpec((None, kp, tn), functools.partial(
            lambda j, i, blk: (layer, blk, j), blk=off // kp)))
        off += kp
    assert off == b_all.shape[1]
    b = b_all
    o_spec = pl.BlockSpec((tm, tn), lambda j, i: (i, j))
    return pl.pallas_call(
        functools.partial(_mm_parts_res_kernel, n_parts=len(parts), tm=tm, nctx=nctx),
        grid=(n // tn, m // tm),
        in_specs=a_specs + b_specs + [o_spec, pl.BlockSpec((2, tn), lambda j, i: (0, j))],
        out_specs=o_spec,
        out_shape=jax.ShapeDtypeStruct((m, n), F32),
        compiler_params=_params("parallel", "parallel"),
        name=name,
    )(*parts, *([b] * len(parts)), res, gate)


def _ssd_chunk_of_step(s, ncc, nch, reverse):
    if not reverse:
        return s
    return jnp.where(s < ncc, ncc - 1 - s, nch - 1 + ncc - s)


def _conv_silu(xbc_ref, prev_ref, next_ref, cw_ref, cb_ref, xw_ref, c, ncc, nch):
    q = SSD_CHUNK
    has_prev = jnp.logical_and(c != 0, c != ncc)
    has_next = jnp.logical_and(c != ncc - 1, c != nch - 1)
    xw_ref[0:CONV_HALO, :] = jnp.where(has_prev, prev_ref[...], 0.0)
    xw_ref[CONV_HALO:CONV_HALO + q, :] = xbc_ref[...]
    xw_ref[CONV_HALO + q:, :] = jnp.where(has_next, next_ref[...], 0.0)
    u = jnp.zeros(xbc_ref.shape, F32) + cb_ref[...]
    for kk in range(CONV_K):
        start = CONV_HALO - CONV_K // 2 + kk
        u = u + xw_ref[start:start + q, :] * cw_ref[kk:kk + 1, :]
    return u * (1.0 / (1.0 + jnp.exp(-u)))


def _ssd_kernel(*refs, reverse, ncc, nch):
    if reverse:
        (u_ref, dt_ref, dtb_ref, a_ref, z_ref, yf_ref, dsk_ref, ng_ref, o_ref, h_ref) = refs
    else:
        (xbc_ref, prev_ref, next_ref, cw_ref, cb_ref, dt_ref, dtb_ref, a_ref,
         o_ref, u_out_ref, h_ref, xw_ref) = refs
    q = SSD_CHUNK
    step = pl.program_id(0)
    n_xs = SSD_GROUPS * 4 * SSD_HEAD_DIM
    n_bc = SSD_GROUPS * SSD_STATE
    n_heads = n_xs // SSD_HEAD_DIM

    @pl.when(step == 0)
    def _():
        h_ref[...] = jnp.zeros_like(h_ref)

    if reverse:
        u = u_ref[...]
    else:
        u = _conv_silu(xbc_ref, prev_ref, next_ref, cw_ref, cb_ref, xw_ref, step, ncc, nch)
        u_out_ref[...] = u

    dtr = dt_ref[...] + dtb_ref[...]
    dt = jnp.maximum(dtr, 0.0) + jnp.log1p(jnp.exp(-jnp.abs(dtr)))
    d_a = dt * a_ref[...]
    ii = lax.broadcasted_iota(jnp.int32, (q, q), 0)
    jj = lax.broadcasted_iota(jnp.int32, (q, q), 1)
    mask = (jj >= ii) if reverse else (jj <= ii)
    tri = jnp.where(mask, 1.0, 0.0).astype(F32)
    cs = jnp.dot(tri, d_a, preferred_element_type=F32, precision=lax.Precision.HIGHEST)
    cs_t = cs.T
    dt_t = dt.T
    last = 0 if reverse else q - 1
    total = cs[last:last + 1, :]
    w_state_t = (dt * jnp.exp(total - cs)).T
    e_cs = jnp.exp(cs)
    chunk_decay = jnp.exp(total)
    lane = lax.broadcasted_iota(jnp.int32, (1, 2 * SSD_HEAD_DIM), 1)
    first_half = lane < SSD_HEAD_DIM
    dir_off = n_heads if reverse else 0

    y_parts = []
    for g in range(SSD_GROUPS):
        bg = u[:, n_xs + g * SSD_STATE:n_xs + (g + 1) * SSD_STATE]
        cg = u[:, n_xs + n_bc + g * SSD_STATE:n_xs + n_bc + (g + 1) * SSD_STATE]
        cb = lax.dot_general(cg.astype(MXU_DTYPE), bg.astype(MXU_DTYPE),
                             (((1,), (1,)), ((), ())), preferred_element_type=F32)
        bg_t = bg.T
        for pair in range(2):
            lo = g * 4 * SSD_HEAD_DIM + pair * 2 * SSD_HEAD_DIM
            xs_pair = u[:, lo:lo + 2 * SSD_HEAD_DIM].astype(MXU_DTYPE)
            h_pair = h_ref[g, :, pair * 128:(pair + 1) * 128]
            rhs = jnp.concatenate([xs_pair, h_pair.astype(MXU_DTYPE)], axis=0)
            ys, sts, decs = [], [], []
            for r in range(2):
                col = dir_off + g * 4 + pair * 2 + r
                seg = jnp.exp(jnp.where(mask, cs[:, col:col + 1] - cs_t[col:col + 1, :], -jnp.inf))
                m_h = cb * seg * dt_t[col:col + 1, :]
                c_h = cg * e_cs[:, col:col + 1]
                lhs = jnp.concatenate([m_h.astype(MXU_DTYPE), c_h.astype(MXU_DTYPE)], axis=1)
                ys.append(jnp.dot(lhs, rhs, preferred_element_type=F32))
                b_h = bg_t * w_state_t[col:col + 1, :]
                sts.append(jnp.dot(b_h.astype(MXU_DTYPE), xs_pair, preferred_element_type=F32))
                decs.append(chunk_decay[:, col:col + 1])
            y_parts.append(jnp.where(first_half, ys[0], ys[1]))
            dec = jnp.where(first_half, decs[0], decs[1])
            h_ref[g, :, pair * 128:(pair + 1) * 128] = (
                h_pair * dec + jnp.where(first_half, sts[0], sts[1]))
    y = jnp.concatenate(y_parts, axis=1)

    if not reverse:
        o_ref[...] = y
    else:
        y = y + yf_ref[...] + dsk_ref[...] * u[:, :n_xs]
        zz = z_ref[...]
        y = y * (zz * (1.0 / (1.0 + jnp.exp(-zz))))
        gw = n_xs // SSD_GROUPS
        outs = []
        for g in range(SSD_GROUPS):
            sl = y[:, g * gw:(g + 1) * gw]
            outs.append(sl * lax.rsqrt(jnp.mean(sl * sl, axis=-1, keepdims=True) + NORM_EPS))
        o_ref[...] = (jnp.concatenate(outs, axis=1) * ng_ref[...]).astype(o_ref.dtype)


def _ssd_forward(zx, p2, conv_w8, conv_b, dtb, a_vec, nctx):
    m = zx.shape[0]
    q = SSD_CHUNK
    nch, ncc = m // q, nctx // q
    n_xs = SSD_GROUPS * 4 * SSD_HEAD_DIM
    n_conv = conv_b.shape[1]
    dt_blk = (p2.shape[1] - 128) // 128
    hb = q // CONV_HALO
    n_hb = m // CONV_HALO
    return pl.pallas_call(
        functools.partial(_ssd_kernel, reverse=False, ncc=ncc, nch=nch),
        grid=(nch,),
        in_specs=[
            pl.BlockSpec((q, n_conv), lambda c: (c, 0)),
            pl.BlockSpec((CONV_HALO, n_conv), lambda c: (jnp.maximum(c * hb - 1, 0), 0)),
            pl.BlockSpec((CONV_HALO, n_conv), lambda c: (jnp.minimum((c + 1) * hb, n_hb - 1), 0)),
            pl.BlockSpec((8, n_conv), lambda c: (0, 0)),
            pl.BlockSpec((1, n_conv), lambda c: (0, 0)),
            pl.BlockSpec((q, 128), lambda c: (c, dt_blk)),
            pl.BlockSpec((1, 128), lambda c: (0, 0)),
            pl.BlockSpec((1, 128), lambda c: (0, 0)),
        ],
        out_specs=[pl.BlockSpec((q, n_xs), lambda c: (c, 0)),
                   pl.BlockSpec((q, n_conv), lambda c: (c, 0))],
        out_shape=[jax.ShapeDtypeStruct((m, n_xs), F32), jax.ShapeDtypeStruct((m, n_conv), F32)],
        scratch_shapes=[
            pltpu.VMEM((SSD_GROUPS, SSD_STATE, 4 * SSD_HEAD_DIM), F32),
            pltpu.VMEM((q + 2 * CONV_HALO, n_conv), F32),
        ],
        compiler_params=_params("arbitrary"),
        name="ssd_fwd",
    )(zx, zx, zx, conv_w8, conv_b, p2, dtb, a_vec)


def _ssd(u, zx, p2, dtb, a_vec, dsk, ng, nctx, *, reverse, yf=None):
    assert reverse
    m = zx.shape[0]
    q = SSD_CHUNK
    nch, ncc = m // q, nctx // q
    n_xs = SSD_GROUPS * 4 * SSD_HEAD_DIM
    n_conv = u.shape[1]
    dt_blk = (p2.shape[1] - 128) // 128

    def cidx(s):
        return _ssd_chunk_of_step(s, ncc, nch, reverse)

    in_specs = [
        pl.BlockSpec((q, n_conv), lambda s: (cidx(s), 0)),
        pl.BlockSpec((q, 128), lambda s: (cidx(s), dt_blk)),
        pl.BlockSpec((1, 128), lambda s: (0, 0)),
        pl.BlockSpec((1, 128), lambda s: (0, 0)),
    ]
    args = [u, p2, dtb, a_vec]
    if reverse:
        in_specs += [
            pl.BlockSpec((q, n_xs), lambda s: (cidx(s), n_conv // n_xs)),
            pl.BlockSpec((q, n_xs), lambda s: (cidx(s), 0)),
            pl.BlockSpec((1, n_xs), lambda s: (0, 0)),
            pl.BlockSpec((1, n_xs), lambda s: (0, 0)),
        ]
        args += [zx, yf, dsk, ng]
    return pl.pallas_call(
        functools.partial(_ssd_kernel, reverse=reverse, ncc=ncc, nch=nch),
        grid=(nch,),
        in_specs=in_specs,
        out_specs=pl.BlockSpec((q, n_xs), lambda s: (cidx(s), 0)),
        out_shape=jax.ShapeDtypeStruct((m, n_xs), MXU_DTYPE if reverse else F32),
        scratch_shapes=[pltpu.VMEM((SSD_GROUPS, SSD_STATE, 4 * SSD_HEAD_DIM), F32)],
        compiler_params=_params("arbitrary"),
        name="ssd_bwd_finish" if reverse else "ssd_fwd",
    )(*args)


_QKV_SLOTS = (
    (8, None, True, True),
    (2, None, True, False),
    (2, None, False, False),
    (8, "q", True, True),
    (2, "k", True, False),
    (2, None, False, False),
)
N_QKV = sum(s[0] for s in _QKV_SLOTS) * HEAD_DIM


GLB_V_SLOT = 22
SWA_V_SLOT = 10
VT_ROWS = HEAD_DIM + 16
KV_CHUNK = 256


def _qkv_kernel(p_ref, cos_ref, sin_ref, qg_ref, kg_ref, o_ref, vt_ref, vts_ref):
    tm = p_ref.shape[0]
    row = lax.broadcasted_iota(jnp.int32, (VT_ROWS - HEAD_DIM, tm), 0)
    ones_rows = jnp.where(row == 0, 1.0, 0.0).astype(vt_ref.dtype)
    for h in range(GLB_KV_HEADS):
        v = p_ref[:, (GLB_V_SLOT + h) * HEAD_DIM:(GLB_V_SLOT + h + 1) * HEAD_DIM]
        vt_ref[h, 0, 0:HEAD_DIM, :] = v.T.astype(vt_ref.dtype)
        vt_ref[h, 0, HEAD_DIM:VT_ROWS, :] = ones_rows
    row_blk = lax.broadcasted_iota(jnp.int32, (VT_ROWS - HEAD_DIM, ATT_BLOCK), 0)
    ones_blk = jnp.where(row_blk == 0, 1.0, 0.0).astype(vts_ref.dtype)
    for h in range(SWA_KV_HEADS):
        for b in range(tm // ATT_BLOCK):
            v = p_ref[b * ATT_BLOCK:(b + 1) * ATT_BLOCK,
                      (SWA_V_SLOT + h) * HEAD_DIM:(SWA_V_SLOT + h + 1) * HEAD_DIM]
            vts_ref[h, b, 0:HEAD_DIM, :] = v.T.astype(vts_ref.dtype)
            vts_ref[h, b, HEAD_DIM:VT_ROWS, :] = ones_blk
    cos = cos_ref[...]
    sin = sin_ref[...]
    lane = lax.broadcasted_iota(jnp.int32, (1, HEAD_DIM), 1)
    lower = (lane % (HEAD_DIM // 2)) < (HEAD_DIM // 4)
    slot = 0
    for count, norm, rope, scale in _QKV_SLOTS:
        for _ in range(count):
            t = p_ref[:, slot * HEAD_DIM:(slot + 1) * HEAD_DIM]
            if norm is not None:
                g = qg_ref[...] if norm == "q" else kg_ref[...]
                t = t * lax.rsqrt(jnp.mean(t * t, axis=-1, keepdims=True) + NORM_EPS) * g
            if rope:
                swapped = jnp.where(lower, pltpu.roll(t, HEAD_DIM - HEAD_DIM // 4, 1),
                                    pltpu.roll(t, HEAD_DIM // 4, 1))
                t = t * cos + swapped * sin
            if scale:
                t = t * (HEAD_DIM ** -0.5 * LOG2_E)
            o_ref[:, slot * HEAD_DIM:(slot + 1) * HEAD_DIM] = t.astype(o_ref.dtype)
            slot += 1


def _qkv_prep(p2, cos, sin, qg, kg):
    m = p2.shape[0]
    tm = KV_CHUNK
    return pl.pallas_call(
        _qkv_kernel,
        grid=(m // tm,),
        in_specs=[
            pl.BlockSpec((tm, N_QKV), lambda i: (i, 0)),
            pl.BlockSpec((tm, HEAD_DIM), lambda i: (i, 0)),
            pl.BlockSpec((tm, HEAD_DIM), lambda i: (i, 0)),
            pl.BlockSpec((1, HEAD_DIM), lambda i: (0, 0)),
            pl.BlockSpec((1, HEAD_DIM), lambda i: (0, 0)),
        ],
        out_specs=[
            pl.BlockSpec((tm, N_QKV), lambda i: (i, 0)),
            pl.BlockSpec((GLB_KV_HEADS, 1, VT_ROWS, tm), lambda i: (0, i, 0, 0)),
            pl.BlockSpec((SWA_KV_HEADS, tm // ATT_BLOCK, VT_ROWS, ATT_BLOCK), lambda i: (0, i, 0, 0)),
        ],
        out_shape=[
            jax.ShapeDtypeStruct((m, N_QKV), MXU_DTYPE),
            jax.ShapeDtypeStruct((GLB_KV_HEADS, m // tm, VT_ROWS, tm), MXU_DTYPE),
            jax.ShapeDtypeStruct((SWA_KV_HEADS, m // ATT_BLOCK, VT_ROWS, ATT_BLOCK), MXU_DTYPE),
        ],
        compiler_params=_params("parallel"),
        name="qkv_prep",
    )(p2, cos, sin, qg.reshape(1, HEAD_DIM), kg.reshape(1, HEAD_DIM))


def _rope_tables(s, nctx):
    t = jnp.arange(s, dtype=jnp.int32)
    pos_r = (t // GRID_W).astype(F32)
    pos_c = (t % GRID_W).astype(F32)
    n_freq = HEAD_DIM // 4
    inv = ROPE_THETA ** (-jnp.arange(n_freq, dtype=F32) / n_freq)
    ar, ac = pos_r[:, None] * inv, pos_c[:, None] * inv
    cos = jnp.concatenate([jnp.cos(ar), jnp.cos(ar), jnp.cos(ac), jnp.cos(ac)], axis=1)
    sin = jnp.concatenate([-jnp.sin(ar), jnp.sin(ar), -jnp.sin(ac), jnp.sin(ac)], axis=1)
    cos = jnp.concatenate([jnp.ones((nctx, HEAD_DIM), F32), cos], axis=0)
    sin = jnp.concatenate([jnp.zeros((nctx, HEAD_DIM), F32), sin], axis=0)
    return cos, sin


def _stack_heads(q, n):
    return jnp.concatenate([q[:, g * HEAD_DIM:(g + 1) * HEAD_DIM] for g in range(n)], axis=0)


def _unstack_heads(o, n, rows):
    return jnp.concatenate([o[g * rows:(g + 1) * rows, :] for g in range(n)], axis=1)


_NT = (((1,), (1,)), ((), ()))


def _glb_kernel(q_ref, k_ref, vt_ref, o_ref, acc_ref, *, tq, nk, unroll, gq):
    tk = KV_CHUNK
    acc_ref[...] = jnp.zeros_like(acc_ref)
    qs = [q_ref[:, g * HEAD_DIM:(g + 1) * HEAD_DIM] for g in range(gq)]

    def body(jj, ms):
        ms = list(ms)

        def scores(c):
            start = pl.multiple_of((jj * unroll + c) * tk, tk)
            kb = k_ref[pl.ds(start, tk), :]
            return [lax.dot_general(kb, q, _NT, preferred_element_type=F32) for q in qs]

        def softmax(s_ts):
            alphas, p_ts = [], []
            for g in range(gq):
                m_new = jnp.maximum(ms[g], jnp.max(s_ts[g], axis=0, keepdims=True))
                alphas.append(jnp.exp2(ms[g] - m_new))
                p_ts.append(jnp.exp2(s_ts[g] - m_new).astype(MXU_DTYPE))
                ms[g] = m_new
            return alphas, p_ts

        def accumulate(c, alphas, p_ts):
            vtb = vt_ref[jj * unroll + c]
            pvs = [jnp.dot(vtb, p_t, preferred_element_type=F32) for p_t in p_ts]
            for g in range(gq):
                acc_ref[g] = alphas[g] * acc_ref[g] + pvs[g]

        s_cur = scores(0)
        s_next = scores(1) if unroll > 1 else None
        prob = softmax(s_cur)
        for c in range(1, unroll):
            s_cur = s_next
            s_next = scores(c + 1) if c + 1 < unroll else None
            accumulate(c - 1, *prob)
            prob = softmax(s_cur)
        accumulate(unroll - 1, *prob)
        return tuple(ms)

    init = tuple(jnp.full((1, tq), NEG_BIG, F32) for _ in range(gq))
    lax.fori_loop(0, nk // unroll, body, init)
    for g in range(gq):
        acc = acc_ref[g]
        out_t = acc[0:HEAD_DIM, :] / acc[HEAD_DIM:HEAD_DIM + 1, :]
        o_ref[:, g * HEAD_DIM:(g + 1) * HEAD_DIM] = out_t.T.astype(o_ref.dtype)


def _glb_attention(qkv, vt, nctx):
    m = qkv.shape[0]
    s = m - nctx
    gq = 8 // GLB_KV_HEADS
    tq = KV_CHUNK
    qw = gq * HEAD_DIM
    q_blk0 = (12 * HEAD_DIM) // qw
    nk = m // KV_CHUNK
    unroll = _pick(nk, (13, 5, 3, 2, 1))
    return pl.pallas_call(
        functools.partial(_glb_kernel, tq=tq, nk=nk, unroll=unroll, gq=gq),
        grid=(GLB_KV_HEADS, s // tq),
        in_specs=[
            pl.BlockSpec((tq, qw), lambda h, i: (i + nctx // tq, q_blk0 + h)),
            pl.BlockSpec((m, HEAD_DIM), lambda h, i: (0, 20 + h)),
            pl.BlockSpec((None, nk, VT_ROWS, KV_CHUNK), lambda h, i: (h, 0, 0, 0)),
        ],
        out_specs=pl.BlockSpec((tq, qw), lambda h, i: (i + nctx // tq, h)),
        out_shape=jax.ShapeDtypeStruct((m, GLB_KV_HEADS * qw), MXU_DTYPE),
        scratch_shapes=[pltpu.VMEM((gq, VT_ROWS, tq), F32)],
        compiler_params=_params("parallel", "parallel"),
        name="global_attention",
    )(qkv, qkv, vt)


def _swa_kernel(q_ref, k_ref, vt_ref, sink_ref, o_ref, *, nctx, nb, gq):
    blk = ATT_BLOCK
    n = pl.program_id(1)
    base = nctx + n * blk
    starts = (pl.multiple_of(base - blk, blk), pl.multiple_of(base, blk),
              pl.multiple_of(jnp.minimum(base + blk, nctx + (nb - 1) * blk), blk))
    kj = lax.broadcasted_iota(jnp.int32, (blk, 2 * blk), 0)
    qi = lax.broadcasted_iota(jnp.int32, (blk, 2 * blk), 1) % blk
    k_parts = [k_ref[pl.ds(st, blk), :] for st in starts] + [k_ref[0:nctx, :]]
    vt = jnp.concatenate([vt_ref[st // blk] for st in starts]
                         + [vt_ref[c] for c in range(nctx // blk)], axis=1)
    pairs = range(gq // 2)
    q2s = [jnp.concatenate([q_ref[:, (2 * pr + r) * HEAD_DIM:(2 * pr + r + 1) * HEAD_DIM]
                            for r in range(2)], axis=0) for pr in pairs]
    scores = [[lax.dot_general(kp, q2, _NT, preferred_element_type=F32) for kp in k_parts]
              for q2 in q2s]
    probs, maxes = [], []
    for pr in pairs:
        s = scores[pr]
        s[0] = jnp.where(jnp.logical_and(kj >= qi, n > 0), s[0], NEG_BIG)
        s[2] = jnp.where(jnp.logical_and(kj <= qi, n < nb - 1), s[2], NEG_BIG)
        m_i = sink_ref[pr]
        for t in s:
            m_i = jnp.maximum(m_i, jnp.max(t, axis=0, keepdims=True))
        probs.append(jnp.concatenate([jnp.exp2(t - m_i).astype(MXU_DTYPE) for t in s], axis=0))
        maxes.append(m_i)
    accs = [jnp.dot(vt, p, preferred_element_type=F32) for p in probs]
    for pr in pairs:
        acc = accs[pr]
        l_i = acc[HEAD_DIM:HEAD_DIM + 1, :] + jnp.exp2(sink_ref[pr] - maxes[pr])
        out = (acc[0:HEAD_DIM, :] / l_i).T
        for r in range(2):
            lo = (2 * pr + r) * HEAD_DIM
            o_ref[:, lo:lo + HEAD_DIM] = out[r * blk:(r + 1) * blk, :].astype(o_ref.dtype)


def _sink_rows(sink, kv_heads, rows_per_head):
    gq = sink.shape[0] // kv_heads
    t = jnp.repeat(sink.reshape(kv_heads, gq).astype(F32) * LOG2_E, rows_per_head, axis=1)
    return jnp.broadcast_to(t[:, :, None], (kv_heads, gq * rows_per_head, 128))


def _swa_attention(qkv, vt, sink, nctx):
    m = qkv.shape[0]
    s = m - nctx
    blk = ATT_BLOCK
    gq = 8 // SWA_KV_HEADS
    nb = s // blk
    qw = gq * HEAD_DIM
    sink_rows = jnp.repeat(sink.astype(F32) * LOG2_E, blk).reshape(SWA_KV_HEADS, gq // 2, 1, 2 * blk)
    return pl.pallas_call(
        functools.partial(_swa_kernel, nctx=nctx, nb=nb, gq=gq),
        grid=(SWA_KV_HEADS, nb),
        in_specs=[
            pl.BlockSpec((blk, qw), lambda h, i: (i + nctx // blk, h)),
            pl.BlockSpec((m, HEAD_DIM), lambda h, i: (0, 8 + h)),
            pl.BlockSpec((None, m // blk, VT_ROWS, blk), lambda h, i: (h, 0, 0, 0)),
            pl.BlockSpec((None, gq // 2, 1, 2 * blk), lambda h, i: (h, 0, 0, 0)),
        ],
        out_specs=pl.BlockSpec((blk, qw), lambda h, i: (i + nctx // blk, h)),
        out_shape=jax.ShapeDtypeStruct((m, SWA_KV_HEADS * qw), MXU_DTYPE),
        compiler_params=_params("parallel", "parallel"),
        name="window_attention",
    )(qkv, qkv, vt, sink_rows)


def _ctx_kernel(*refs, gq, nctx, has_sink):
    if has_sink:
        q_ref, k_ref, v_ref, sink_ref, _, o_ref = refs
    else:
        q_ref, k_ref, v_ref, _, o_ref = refs
    qs = _stack_heads(q_ref[...], gq)
    s = lax.dot_general(qs, k_ref[...], _NT, preferred_element_type=F32)
    m_i = jnp.max(s, axis=-1, keepdims=True)
    if has_sink:
        sink = sink_ref[:, 0:1]
        m_i = jnp.maximum(m_i, sink)
    p = jnp.exp2(s - m_i)
    l_i = jnp.sum(p, axis=-1, keepdims=True)
    if has_sink:
        l_i = l_i + jnp.exp2(sink - m_i)
    acc = jnp.dot(p.astype(MXU_DTYPE), v_ref[...], preferred_element_type=F32)
    o_ref[...] = _unstack_heads(acc / l_i, gq, nctx).astype(o_ref.dtype)


def _ctx_attention(qkv, dst, nctx, q_slot, k_slot, v_slot, sink):
    gq = 4
    qw = gq * HEAD_DIM
    has_sink = sink is not None
    in_specs = [
        pl.BlockSpec((nctx, qw), lambda h: (0, q_slot // gq + h)),
        pl.BlockSpec((nctx, HEAD_DIM), lambda h: (0, k_slot + h)),
        pl.BlockSpec((nctx, HEAD_DIM), lambda h: (0, v_slot + h)),
    ]
    args = [qkv, qkv, qkv]
    if has_sink:
        in_specs.append(pl.BlockSpec((None, gq * nctx, 128), lambda h: (h, 0, 0)))
        args.append(_sink_rows(sink, 2, nctx))
    in_specs.append(pl.BlockSpec(memory_space=pl.ANY))
    args.append(dst)
    return pl.pallas_call(
        functools.partial(_ctx_kernel, gq=gq, nctx=nctx, has_sink=has_sink),
        grid=(2,),
        in_specs=in_specs,
        out_specs=pl.BlockSpec((nctx, qw), lambda h: (0, h)),
        out_shape=jax.ShapeDtypeStruct(dst.shape, dst.dtype),
        input_output_aliases={len(args) - 1: 0},
        compiler_params=_params("parallel"),
        name="context_attention",
    )(*args)


def _top_rows(s, count):
    rows = []
    for _ in range(count):
        mx = jnp.max(s, axis=0, keepdims=True)
        rows.append(mx)
        s = jnp.where(s == mx, NEG_BIG, s)
    return rows


def _route_kernel(q_ref, keys_ref, o_ref, *, n_keys, heads):
    for hh in range(heads):
        _route_one_head(q_ref, keys_ref, o_ref, hh, n_keys)


def _route_one_head(q_ref, keys_ref, o_ref, hh, n_keys):
    tops = []
    for half in range(2):
        lo_lane = (2 * hh + half) * PEER_HALF
        qh = q_ref[:, lo_lane:lo_lane + PEER_HALF]
        s_t = lax.dot_general(keys_ref[hh, half], qh, _NT, preferred_element_type=F32)
        tops.append(_top_rows(s_t[0:n_keys, :], PEER_TOPK + 1))
    a_top, b_top = tops
    a_all = jnp.concatenate(a_top[:PEER_TOPK], axis=0)
    b_all = jnp.concatenate(b_top[:PEER_TOPK], axis=0)
    lo, hi = slice(0, 8), slice(8, PEER_TOPK)
    upper = lax.broadcasted_iota(jnp.int32, (8, 1), 0) >= 4
    cand = jnp.concatenate(
        [a_top[0] + b_all[lo], a_top[0] + b_all[hi], b_top[0] + a_all[hi]]
        + [a_top[i] + b_all[lo] for i in (1, 2, 3)]
        + [jnp.where(upper, b_top[j] + a_all[lo], NEG_BIG) for j in (0, 1, 2)], axis=0)
    c = _top_rows(cand, PEER_TOPK + 1)
    runner_up = jnp.maximum(c[PEER_TOPK], jnp.maximum(a_top[PEER_TOPK] + b_top[0],
                                                       a_top[0] + b_top[PEER_TOPK]))
    thr = 0.5 * (c[PEER_TOPK - 1] + runner_up)
    z = jnp.zeros_like(thr)
    for kk in range(PEER_TOPK):
        z = z + jnp.exp(c[kk] - c[0])
    pad = jnp.zeros((6, thr.shape[1]), F32)
    o_ref[hh] = jnp.concatenate([thr, c[0] + jnp.log(z), pad], axis=0)


def _peer_route(q, keys_pad, n_keys):
    m = q.shape[0]
    tm = _pick(m, (256, 128))
    heads = ROUTE_HEADS_PER_STEP
    return pl.pallas_call(
        functools.partial(_route_kernel, n_keys=n_keys, heads=heads),
        grid=(m // tm, PEER_HEADS // heads),
        in_specs=[
            pl.BlockSpec((tm, heads * 2 * PEER_HALF), lambda i, h: (i, h)),
            pl.BlockSpec((heads, 2, 128, PEER_HALF), lambda i, h: (h, 0, 0, 0)),
        ],
        out_specs=pl.BlockSpec((heads, 8, tm), lambda i, h: (h, 0, i)),
        out_shape=jax.ShapeDtypeStruct((PEER_HEADS, 8, m), F32),
        compiler_params=_params("parallel", "parallel"),
        name="peer_route",
    )(q, keys_pad)


def _peer_w_kernel(h_ref, u_ref, q_ref, kx_ref, r_ref, o_ref, *, slab):
    kd = 2 * PEER_HALF
    for c in range(o_ref.shape[1] // slab):
        sl = slice(c * slab, (c + 1) * slab)
        a = lax.dot_general(h_ref[...], u_ref[sl, :], _NT, preferred_element_type=F32)
        act = 0.5 * a * (1.0 + lax.erf(a * (2.0 ** -0.5)))
        g = None
        for h in range(PEER_HEADS):
            s = jnp.dot(q_ref[:, h * kd:(h + 1) * kd], kx_ref[h, :, sl],
                        preferred_element_type=F32)
            thr = r_ref[:, h:h + 1]
            log_norm = r_ref[:, PEER_HEADS + h:PEER_HEADS + h + 1]
            t = jnp.where(s >= thr, jnp.exp(s - log_norm), 0.0)
            g = t if g is None else g + t
        o_ref[:, sl] = (g * act).astype(o_ref.dtype)


def _peer_weights(h2, u_all, layer, q, kexp, route):
    m, d = h2.shape
    n_exp = u_all.shape[1]
    tm = _pick(m, (256, 128))
    slab = _pick(n_exp, (256, 128))
    te = _pick(n_exp, (5 * slab, slab))
    return pl.pallas_call(
        functools.partial(_peer_w_kernel, slab=slab),
        grid=(n_exp // te, m // tm),
        in_specs=[
            pl.BlockSpec((tm, d), lambda e, i: (i, 0)),
            pl.BlockSpec((None, te, d), lambda e, i: (layer, e, 0)),
            pl.BlockSpec((tm, q.shape[1]), lambda e, i: (i, 0)),
            pl.BlockSpec((PEER_HEADS, 2 * PEER_HALF, te), lambda e, i: (0, 0, e)),
            pl.BlockSpec((tm, route.shape[1]), lambda e, i: (i, 0)),
        ],
        out_specs=pl.BlockSpec((tm, te), lambda e, i: (i, e)),
        out_shape=jax.ShapeDtypeStruct((m, n_exp), MXU_DTYPE),
        compiler_params=_params("parallel", "parallel"),
        name="peer_expert_weights",
    )(h2, u_all, q, kexp, route)


def _peer_key_tables(keys):
    nh, _, nk, kd = keys.shape
    keys_pad = jnp.zeros((nh, 2, 128, kd), keys.dtype).at[:, :, :nk].set(keys)
    k1 = jnp.repeat(keys[:, 0], nk, axis=1)
    k2 = jnp.tile(keys[:, 1], (1, nk, 1))
    kexp = jnp.swapaxes(jnp.concatenate([k1, k2], axis=-1), 1, 2)
    return keys_pad.astype(MXU_DTYPE), kexp.astype(MXU_DTYPE)


def kernel(x, c, ctx, c_ctx, ada_w, ada_b, norm1_g, w_in, conv_w, conv_b, dt_bias, a_log, d_skip,
           ssd_norm_g, swa_sink, q_norm_g, k_norm_g, w_out, norm2_g, peer_wq, peer_keys, peer_u,
           peer_v, final_g):
    assert x.shape[0] == 1 and c.shape[0] == 1 and ctx.shape[0] == 1
    s, d = x.shape[1], x.shape[2]
    nctx = ctx.shape[1]
    depth = ada_w.shape[0]
    n_heads = d_skip.shape[1]
    n_xs = n_heads * SSD_HEAD_DIM
    n_conv = conv_w.shape[2]
    n_keys = peer_keys.shape[3]
    assert n_xs == SSD_GROUPS * 4 * SSD_HEAD_DIM and n_conv == n_xs + 2 * SSD_GROUPS * SSD_STATE
    assert nctx % 256 == 0 and s % 256 == 0 and peer_keys.shape[4] == PEER_HALF
    bf = MXU_DTYPE

    cvec = jnp.zeros((8, d), F32).at[0].set(c_ctx).at[1].set(c[0])
    mods = _mods(cvec, ada_w, ada_b)[:, 0:2].reshape(depth, 2, 6, d)
    cos, sin = _rope_tables(s, nctx)
    xs = jnp.concatenate([ctx[0], x[0]], axis=0)

    o_z, o_xbc, o_dt = 0, n_xs, n_xs + n_conv
    o_att = o_dt + 2 * n_heads
    pad_dt = jnp.zeros((d, 128 - 2 * n_heads), F32)
    w_out_b, u_b, v_b = (t.astype(bf) for t in (w_out, peer_u, peer_v))

    for l in range(depth):
        ml = [mods[l, :, i] for i in range(6)]
        w = w_in[l]
        w_b = jnp.concatenate([w[:, o_att:], w[:, o_dt:o_att], pad_dt], axis=1).astype(bf)

        h1 = _normmod(xs, norm1_g[l], ml[0], ml[1], nctx)
        zx = _matmul_f32w(
            h1, w_in, l, F32, n_out=n_conv + n_xs, name="in_proj_ssd",
            col_block=lambda j, tn: jnp.where(j < n_conv // tn, j + n_xs // tn, j - n_conv // tn))
        p2 = _matmul(h1, w_b, F32, name="in_proj_attn")

        conv_w8 = jnp.zeros((8, n_conv), F32).at[:CONV_K].set(conv_w[l])
        zpad = jnp.zeros((128 - 2 * n_heads,), F32)
        dtb = jnp.concatenate([dt_bias[l].reshape(-1), zpad]).reshape(1, 128)
        a_vec = jnp.concatenate([-jnp.exp(a_log[l].reshape(-1)), zpad]).reshape(1, 128)
        dsk = jnp.repeat(d_skip[l], SSD_HEAD_DIM).reshape(1, n_xs)
        ng = ssd_norm_g[l].reshape(1, n_xs)
        cb = conv_b[l].reshape(1, n_conv)
        yf, u = _ssd_forward(zx, p2, conv_w8, cb, dtb, a_vec, nctx)
        ssd_out = _ssd(u, zx, p2, dtb, a_vec, dsk, ng, nctx, reverse=True, yf=yf)

        qkv, vt, vt_swa = _qkv_prep(p2, cos, sin, q_norm_g[l], k_norm_g[l])
        swa_out = _swa_attention(qkv, vt_swa, swa_sink[l], nctx)
        swa_out = _ctx_attention(qkv, swa_out, nctx, 0, 8, 10, swa_sink[l])
        glb_out = _glb_attention(qkv, vt, nctx)
        glb_out = _ctx_attention(qkv, glb_out, nctx, 12, 20, GLB_V_SLOT, None)
        xs = _matmul_parts([ssd_out, swa_out, glb_out], w_out_b, l,
                           res=xs, gate=ml[2], nctx=nctx, name="out_proj")

        h2 = _normmod(xs, norm2_g[l], ml[3], ml[4], nctx)
        q = _matmul_f32w(h2, peer_wq, l, bf, n_out=peer_wq.shape[2], name="peer_query",
                         col_block=lambda j, tn: j)
        keys_pad, kexp = _peer_key_tables(peer_keys[l])
        route = _peer_route(q, keys_pad, n_keys)
        route = jnp.transpose(route[:, 0:2], (2, 1, 0)).reshape(nctx + s, 2 * PEER_HEADS)
        wts = _peer_weights(h2, u_b, l, q, kexp, route)
        xs = _matmul(wts, v_b, F32, layer=l, res=xs, gate=ml[5], nctx=nctx, name="peer_out")

    return _finalnorm(xs, final_g, nctx)[None]
```

```python
import functools
import math

import jax
import jax.numpy as jnp
import numpy as np
from jax import lax
from jax.experimental import pallas as pl
from jax.experimental.pallas import tpu as pltpu

F32 = jnp.float32
MXU_DTYPE = jnp.bfloat16

HEAD_DIM = 128
GRID_W = 64
ROPE_THETA = 10000.0
NORM_EPS = 1e-6

SSD_HEAD_DIM = 64
SSD_GROUPS = 8
SSD_STATE = 128
SSD_CHUNK = 128
CONV_K = 5
CONV_HALO = 8

SWA_KV_HEADS = 2
GLB_KV_HEADS = 2
ATT_BLOCK = 128

PEER_HEADS = 8
PEER_HALF = 128
PEER_TOPK = 16
ROUTE_HEADS_PER_STEP = 4

NEG_BIG = -1e30
LOG2_E = 1.4426950408889634
VMEM_LIMIT = 56 * 1024 * 1024


def _pick(n, candidates):
    for c in candidates:
        if n % c == 0:
            return c
    raise ValueError(f"no tile in {candidates} divides {n}")


def _params(*sem):
    return pltpu.CompilerParams(dimension_semantics=sem, vmem_limit_bytes=VMEM_LIMIT)


def _row_select(ref, row0, tm, nctx):
    rows = row0 + lax.broadcasted_iota(jnp.int32, (tm, 1), 0)
    return jnp.where(rows < nctx, ref[0:1, :], ref[1:2, :])


def _mods_kernel(c_ref, w_ref, b_ref, o_ref):
    cv = c_ref[...]
    sv = cv * (1.0 / (1.0 + jnp.exp(-cv)))
    acc = jnp.dot(sv.astype(MXU_DTYPE), w_ref[...].astype(MXU_DTYPE),
                  preferred_element_type=F32)
    o_ref[...] = acc + b_ref[...]


def _mods(cvec, ada_w, ada_b):
    depth, d, n = ada_w.shape
    tn = _pick(n, (512, 256, 128))
    return pl.pallas_call(
        _mods_kernel,
        grid=(depth, n // tn),
        in_specs=[
            pl.BlockSpec((8, d), lambda l, j: (0, 0)),
            pl.BlockSpec((None, d, tn), lambda l, j: (l, 0, j)),
            pl.BlockSpec((None, 1, tn), lambda l, j: (l, 0, j)),
        ],
        out_specs=pl.BlockSpec((None, 8, tn), lambda l, j: (l, 0, j)),
        out_shape=jax.ShapeDtypeStruct((depth, 8, n), F32),
        compiler_params=_params("parallel", "parallel"),
        name="adaln_mods",
    )(cvec, ada_w, ada_b.reshape(depth, 1, n))


NORM_ROWS = 16
NORM_UNROLL = 8


def _normmod_kernel(x_ref, g_ref, shift_ref, scale_ref, o_ref, gain_ref, bias_ref, *, tm, nctx):
    cls = jnp.where(pl.program_id(0) * tm < nctx, 0, 1)
    d = x_ref.shape[1]
    gain = g_ref[...] * (1.0 + scale_ref[pl.ds(cls, 1), :])
    gain_ref[...] = jnp.broadcast_to(gain, (NORM_ROWS, d))
    bias_ref[...] = jnp.broadcast_to(shift_ref[pl.ds(cls, 1), :], (NORM_ROWS, d))

    def body(r, carry):
        rows = pl.ds(pl.multiple_of(r * NORM_ROWS, NORM_ROWS), NORM_ROWS)
        x = x_ref[rows, :]
        inv = lax.rsqrt(jnp.mean(x * x, axis=-1, keepdims=True) + NORM_EPS)
        y = x_ref[rows, :] * inv * gain_ref[...] + bias_ref[...]
        o_ref[rows, :] = y.astype(o_ref.dtype)
        return carry

    lax.fori_loop(0, tm // NORM_ROWS, body, 0, unroll=NORM_UNROLL)


def _normmod(xs, g, shift, scale, nctx):
    m, d = xs.shape
    tm = _pick(math.gcd(m, nctx), (256, 128))
    return pl.pallas_call(
        functools.partial(_normmod_kernel, tm=tm, nctx=nctx),
        grid=(m // tm,),
        in_specs=[
            pl.BlockSpec((tm, d), lambda i: (i, 0)),
            pl.BlockSpec((1, d), lambda i: (0, 0)),
            pl.BlockSpec((2, d), lambda i: (0, 0)),
            pl.BlockSpec((2, d), lambda i: (0, 0)),
        ],
        out_specs=pl.BlockSpec((tm, d), lambda i: (i, 0)),
        out_shape=jax.ShapeDtypeStruct((m, d), MXU_DTYPE),
        scratch_shapes=[pltpu.VMEM((NORM_ROWS, d), F32), pltpu.VMEM((NORM_ROWS, d), F32)],
        compiler_params=_params("parallel"),
        name="norm_modulate",
    )(xs, g.reshape(1, d), shift, scale)


def _finalnorm_kernel(x_ref, g_ref, o_ref):
    def body(r, carry):
        rows = pl.ds(pl.multiple_of(r * NORM_ROWS, NORM_ROWS), NORM_ROWS)
        x = x_ref[rows, :]
        inv = lax.rsqrt(jnp.mean(x * x, axis=-1, keepdims=True) + NORM_EPS)
        o_ref[rows, :] = x_ref[rows, :] * inv * g_ref[...]
        return carry

    lax.fori_loop(0, x_ref.shape[0] // NORM_ROWS, body, 0, unroll=NORM_UNROLL)


def _finalnorm(xs, g, nctx):
    m, d = xs.shape
    s = m - nctx
    tm = _pick(math.gcd(s, nctx), (256, 128))
    off = nctx // tm
    return pl.pallas_call(
        _finalnorm_kernel,
        grid=(s // tm,),
        in_specs=[
            pl.BlockSpec((tm, d), lambda i: (i + off, 0)),
            pl.BlockSpec((1, d), lambda i: (0, 0)),
        ],
        out_specs=pl.BlockSpec((tm, d), lambda i: (i, 0)),
        out_shape=jax.ShapeDtypeStruct((s, d), F32),
        compiler_params=_params("parallel"),
        name="final_norm",
    )(xs, g.reshape(1, d))


def _mm_kernel(a_ref, b_ref, o_ref):
    o_ref[...] = jnp.dot(a_ref[...], b_ref[...],
                         preferred_element_type=F32).astype(o_ref.dtype)


def _mm_cast_kernel(a_ref, b_ref, o_ref, b_scr):
    @pl.when(pl.program_id(1) == 0)
    def _():
        b_scr[...] = b_ref[...].astype(b_scr.dtype)

    o_ref[...] = jnp.dot(a_ref[...], b_scr[...], preferred_element_type=F32).astype(o_ref.dtype)


def _matmul_f32w(a, b_all, layer, out_dtype, *, n_out, col_block, name):
    m, k = a.shape
    tm = _pick(m, (640, 256, 128))
    tn = _pick(n_out, (512, 256, 128))
    return pl.pallas_call(
        _mm_cast_kernel,
        grid=(n_out // tn, m // tm),
        in_specs=[
            pl.BlockSpec((tm, k), lambda j, i: (i, 0)),
            pl.BlockSpec((None, k, tn), lambda j, i: (layer, 0, col_block(j, tn))),
        ],
        out_specs=pl.BlockSpec((tm, tn), lambda j, i: (i, j)),
        out_shape=jax.ShapeDtypeStruct((m, n_out), out_dtype),
        scratch_shapes=[pltpu.VMEM((k, tn), MXU_DTYPE)],
        compiler_params=_params("parallel", "arbitrary"),
        name=name,
    )(a, b_all)


def _mm_res_kernel(a_ref, b_ref, res_ref, gate_ref, o_ref, *, tm, nctx):
    acc = jnp.dot(a_ref[...], b_ref[...], preferred_element_type=F32)
    gate = _row_select(gate_ref, pl.program_id(1) * tm, tm, nctx)
    o_ref[...] = res_ref[...] + gate * acc


def _matmul(a, b, out_dtype, *, layer=None, res=None, gate=None, nctx=0, name="matmul"):
    m, k = a.shape
    n = b.shape[-1]
    tm = _pick(m, (640, 256, 128))
    tn = _pick(n, (512, 640, 256, 128))
    a_spec = pl.BlockSpec((tm, k), lambda j, i: (i, 0))
    if layer is None:
        b_spec = pl.BlockSpec((k, tn), lambda j, i: (0, j))
    else:
        b_spec = pl.BlockSpec((None, k, tn), lambda j, i: (layer, 0, j))
    o_spec = pl.BlockSpec((tm, tn), lambda j, i: (i, j))
    if res is None:
        kern, specs, args = _mm_kernel, [a_spec, b_spec], (a, b)
    else:
        kern = functools.partial(_mm_res_kernel, tm=tm, nctx=nctx)
        specs = [a_spec, b_spec, o_spec, pl.BlockSpec((2, tn), lambda j, i: (0, j))]
        args = (a, b, res, gate)
    return pl.pallas_call(
        kern,
        grid=(n // tn, m // tm),
        in_specs=specs,
        out_specs=o_spec,
        out_shape=jax.ShapeDtypeStruct((m, n), out_dtype),
        compiler_params=_params("parallel", "parallel"),
        name=name,
    )(*args)


def _mm_parts_res_kernel(*refs, n_parts, tm, nctx):
    a_refs, b_refs = refs[:n_parts], refs[n_parts:2 * n_parts]
    res_ref, gate_ref, o_ref = refs[2 * n_parts:]
    acc = jnp.dot(a_refs[0][...], b_refs[0][...], preferred_element_type=F32)
    for a_ref, b_ref in zip(a_refs[1:], b_refs[1:]):
        acc = acc + jnp.dot(a_ref[...], b_ref[...], preferred_element_type=F32)
    gate = _row_select(gate_ref, pl.program_id(1) * tm, tm, nctx)
    o_ref[...] = res_ref[...] + gate * acc


def _matmul_parts(parts, b_all, layer, *, res, gate, nctx, name):
    m = parts[0].shape[0]
    n = b_all.shape[2]
    tm = _pick(m, (640, 256, 128))
    tn = _pick(n, (512, 256, 128))
    a_specs, b_specs, off = [], [], 0
    for p in parts:
        kp = p.shape[1]
        assert off % kp == 0, "each part's rows of b must start on a multiple of its width"
        a_specs.append(pl.BlockSpec((tm, kp), lambda j, i: (i, 0)))
        b_specs.append(pl.BlockSpec((None, kp, tn), functools.partial(
            lambda j, i, blk: (layer, blk, j), blk=off // kp)))
        off += kp
    assert off == b_all.shape[1]
    b = b_all
    o_spec = pl.BlockSpec((tm, tn), lambda j, i: (i, j))
    return pl.pallas_call(
        functools.partial(_mm_parts_res_kernel, n_parts=len(parts), tm=tm, nctx=nctx),
        grid=(n // tn, m // tm),
        in_specs=a_specs + b_specs + [o_spec, pl.BlockSpec((2, tn), lambda j, i: (0, j))],
        out_specs=o_spec,
        out_shape=jax.ShapeDtypeStruct((m, n), F32),
        compiler_params=_params("parallel", "parallel"),
        name=name,
    )(*parts, *([b] * len(parts)), res, gate)


def _ssd_chunk_of_step(s, ncc, nch, reverse):
    if not reverse:
        return s
    return jnp.where(s < ncc, ncc - 1 - s, nch - 1 + ncc - s)


def _conv_silu(xbc_ref, prev_ref, next_ref, cw_ref, cb_ref, xw_ref, c, ncc, nch):
    q = SSD_CHUNK
    has_prev = jnp.logical_and(c != 0, c != ncc)
    has_next = jnp.logical_and(c != ncc - 1, c != nch - 1)
    xw_ref[0:CONV_HALO, :] = jnp.where(has_prev, prev_ref[...], 0.0)
    xw_ref[CONV_HALO:CONV_HALO + q, :] = xbc_ref[...]
    xw_ref[CONV_HALO + q:, :] = jnp.where(has_next, next_ref[...], 0.0)
    u = jnp.zeros(xbc_ref.shape, F32) + cb_ref[...]
    for kk in range(CONV_K):
        start = CONV_HALO - CONV_K // 2 + kk
        u = u + xw_ref[start:start + q, :] * cw_ref[kk:kk + 1, :]
    return u * (1.0 / (1.0 + jnp.exp(-u)))


def _ssd_kernel(*refs, reverse, ncc, nch):
    if reverse:
        (u_ref, dt_ref, dtb_ref, a_ref, z_ref, yf_ref, dsk_ref, ng_ref, o_ref, h_ref) = refs
    else:
        (xbc_ref, prev_ref, next_ref, cw_ref, cb_ref, dt_ref, dtb_ref, a_ref,
         o_ref, u_out_ref, h_ref, xw_ref) = refs
    q = SSD_CHUNK
    step = pl.program_id(0)
    n_xs = SSD_GROUPS * 4 * SSD_HEAD_DIM
    n_bc = SSD_GROUPS * SSD_STATE
    n_heads = n_xs // SSD_HEAD_DIM

    @pl.when(step == 0)
    def _():
        h_ref[...] = jnp.zeros_like(h_ref)

    if reverse:
        u = u_ref[...]
    else:
        u = _conv_silu(xbc_ref, prev_ref, next_ref, cw_ref, cb_ref, xw_ref, step, ncc, nch)
        u_out_ref[...] = u

    dtr = dt_ref[...] + dtb_ref[...]
    dt = jnp.maximum(dtr, 0.0) + jnp.log1p(jnp.exp(-jnp.abs(dtr)))
    d_a = dt * a_ref[...]
    ii = lax.broadcasted_iota(jnp.int32, (q, q), 0)
    jj = lax.broadcasted_iota(jnp.int32, (q, q), 1)
    mask = (jj >= ii) if reverse else (jj <= ii)
    tri = jnp.where(mask, 1.0, 0.0).astype(F32)
    cs = jnp.dot(tri, d_a, preferred_element_type=F32, precision=lax.Precision.HIGHEST)
    cs_t = cs.T
    dt_t = dt.T
    last = 0 if reverse else q - 1
    total = cs[last:last + 1, :]
    w_state_t = (dt * jnp.exp(total - cs)).T
    e_cs = jnp.exp(cs)
    chunk_decay = jnp.exp(total)
    lane = lax.broadcasted_iota(jnp.int32, (1, 2 * SSD_HEAD_DIM), 1)
    first_half = lane < SSD_HEAD_DIM
    dir_off = n_heads if reverse else 0

    y_parts = []
    for g in range(SSD_GROUPS):
        bg = u[:, n_xs + g * SSD_STATE:n_xs + (g + 1) * SSD_STATE]
        cg = u[:, n_xs + n_bc + g * SSD_STATE:n_xs + n_bc + (g + 1) * SSD_STATE]
        cb = lax.dot_general(cg.astype(MXU_DTYPE), bg.astype(MXU_DTYPE),
                             (((1,), (1,)), ((), ())), preferred_element_type=F32)
        bg_t = bg.T
        for pair in range(2):
            lo = g * 4 * SSD_HEAD_DIM + pair * 2 * SSD_HEAD_DIM
            xs_pair = u[:, lo:lo + 2 * SSD_HEAD_DIM].astype(MXU_DTYPE)
            h_pair = h_ref[g, :, pair * 128:(pair + 1) * 128]
            rhs = jnp.concatenate([xs_pair, h_pair.astype(MXU_DTYPE)], axis=0)
            ys, sts, decs = [], [], []
            for r in range(2):
                col = dir_off + g * 4 + pair * 2 + r
                seg = jnp.exp(jnp.where(mask, cs[:, col:col + 1] - cs_t[col:col + 1, :], -jnp.inf))
                m_h = cb * seg * dt_t[col:col + 1, :]
                c_h = cg * e_cs[:, col:col + 1]
                lhs = jnp.concatenate([m_h.astype(MXU_DTYPE), c_h.astype(MXU_DTYPE)], axis=1)
                ys.append(jnp.dot(lhs, rhs, preferred_element_type=F32))
                b_h = bg_t * w_state_t[col:col + 1, :]
                sts.append(jnp.dot(b_h.astype(MXU_DTYPE), xs_pair, preferred_element_type=F32))
                decs.append(chunk_decay[:, col:col + 1])
            y_parts.append(jnp.where(first_half, ys[0], ys[1]))
            dec = jnp.where(first_half, decs[0], decs[1])
            h_ref[g, :, pair * 128:(pair + 1) * 128] = (
                h_pair * dec + jnp.where(first_half, sts[0], sts[1]))
    y = jnp.concatenate(y_parts, axis=1)

    if not reverse:
        o_ref[...] = y
    else:
        y = y + yf_ref[...] + dsk_ref[...] * u[:, :n_xs]
        zz = z_ref[...]
        y = y * (zz * (1.0 / (1.0 + jnp.exp(-zz))))
        gw = n_xs // SSD_GROUPS
        outs = []
        for g in range(SSD_GROUPS):
            sl = y[:, g * gw:(g + 1) * gw]
            outs.append(sl * lax.rsqrt(jnp.mean(sl * sl, axis=-1, keepdims=True) + NORM_EPS))
        o_ref[...] = (jnp.concatenate(outs, axis=1) * ng_ref[...]).astype(o_ref.dtype)


def _ssd_forward(zx, p2, conv_w8, conv_b, dtb, a_vec, nctx):
    m = zx.shape[0]
    q = SSD_CHUNK
    nch, ncc = m // q, nctx // q
    n_xs = SSD_GROUPS * 4 * SSD_HEAD_DIM
    n_conv = conv_b.shape[1]
    dt_blk = (p2.shape[1] - 128) // 128
    hb = q // CONV_HALO
    n_hb = m // CONV_HALO
    return pl.pallas_call(
        functools.partial(_ssd_kernel, reverse=False, ncc=ncc, nch=nch),
        grid=(nch,),
        in_specs=[
            pl.BlockSpec((q, n_conv), lambda c: (c, 0)),
            pl.BlockSpec((CONV_HALO, n_conv), lambda c: (jnp.maximum(c * hb - 1, 0), 0)),
            pl.BlockSpec((CONV_HALO, n_conv), lambda c: (jnp.minimum((c + 1) * hb, n_hb - 1), 0)),
            pl.BlockSpec((8, n_conv), lambda c: (0, 0)),
            pl.BlockSpec((1, n_conv), lambda c: (0, 0)),
            pl.BlockSpec((q, 128), lambda c: (c, dt_blk)),
            pl.BlockSpec((1, 128), lambda c: (0, 0)),
            pl.BlockSpec((1, 128), lambda c: (0, 0)),
        ],
        out_specs=[pl.BlockSpec((q, n_xs), lambda c: (c, 0)),
                   pl.BlockSpec((q, n_conv), lambda c: (c, 0))],
        out_shape=[jax.ShapeDtypeStruct((m, n_xs), F32), jax.ShapeDtypeStruct((m, n_conv), F32)],
        scratch_shapes=[
            pltpu.VMEM((SSD_GROUPS, SSD_STATE, 4 * SSD_HEAD_DIM), F32),
            pltpu.VMEM((q + 2 * CONV_HALO, n_conv), F32),
        ],
        compiler_params=_params("arbitrary"),
        name="ssd_fwd",
    )(zx, zx, zx, conv_w8, conv_b, p2, dtb, a_vec)


def _ssd(u, zx, p2, dtb, a_vec, dsk, ng, nctx, *, reverse, yf=None):
    assert reverse
    m = zx.shape[0]
    q = SSD_CHUNK
    nch, ncc = m // q, nctx // q
    n_xs = SSD_GROUPS * 4 * SSD_HEAD_DIM
    n_conv = u.shape[1]
    dt_blk = (p2.shape[1] - 128) // 128

    def cidx(s):
        return _ssd_chunk_of_step(s, ncc, nch, reverse)

    in_specs = [
        pl.BlockSpec((q, n_conv), lambda s: (cidx(s), 0)),
        pl.BlockSpec((q, 128), lambda s: (cidx(s), dt_blk)),
        pl.BlockSpec((1, 128), lambda s: (0, 0)),
        pl.BlockSpec((1, 128), lambda s: (0, 0)),
    ]
    args = [u, p2, dtb, a_vec]
    if reverse:
        in_specs += [
            pl.BlockSpec((q, n_xs), lambda s: (cidx(s), n_conv // n_xs)),
            pl.BlockSpec((q, n_xs), lambda s: (cidx(s), 0)),
            pl.BlockSpec((1, n_xs), lambda s: (0, 0)),
            pl.BlockSpec((1, n_xs), lambda s: (0, 0)),
        ]
        args += [zx, yf, dsk, ng]
    return pl.pallas_call(
        functools.partial(_ssd_kernel, reverse=reverse, ncc=ncc, nch=nch),
        grid=(nch,),
        in_specs=in_specs,
        out_specs=pl.BlockSpec((q, n_xs), lambda s: (cidx(s), 0)),
        out_shape=jax.ShapeDtypeStruct((m, n_xs), MXU_DTYPE if reverse else F32),
        scratch_shapes=[pltpu.VMEM((SSD_GROUPS, SSD_STATE, 4 * SSD_HEAD_DIM), F32)],
        compiler_params=_params("arbitrary"),
        name="ssd_bwd_finish" if reverse else "ssd_fwd",
    )(*args)


_QKV_SLOTS = (
    (8, None, True, True),
    (2, None, True, False),
    (2, None, False, False),
    (8, "q", True, True),
    (2, "k", True, False),
    (2, None, False, False),
)
N_QKV = sum(s[0] for s in _QKV_SLOTS) * HEAD_DIM


GLB_V_SLOT = 22
SWA_V_SLOT = 10
VT_ROWS = HEAD_DIM + 16
KV_CHUNK = 256


def _qkv_kernel(p_ref, cos_ref, sin_ref, qg_ref, kg_ref, o_ref, vt_ref, vts_ref):
    tm = p_ref.shape[0]
    row = lax.broadcasted_iota(jnp.int32, (VT_ROWS - HEAD_DIM, tm), 0)
    ones_rows = jnp.where(row == 0, 1.0, 0.0).astype(vt_ref.dtype)
    for h in range(GLB_KV_HEADS):
        v = p_ref[:, (GLB_V_SLOT + h) * HEAD_DIM:(GLB_V_SLOT + h + 1) * HEAD_DIM]
        vt_ref[h, 0, 0:HEAD_DIM, :] = v.T.astype(vt_ref.dtype)
        vt_ref[h, 0, HEAD_DIM:VT_ROWS, :] = ones_rows
    row_blk = lax.broadcasted_iota(jnp.int32, (VT_ROWS - HEAD_DIM, ATT_BLOCK), 0)
    ones_blk = jnp.where(row_blk == 0, 1.0, 0.0).astype(vts_ref.dtype)
    for h in range(SWA_KV_HEADS):
        for b in range(tm // ATT_BLOCK):
            v = p_ref[b * ATT_BLOCK:(b + 1) * ATT_BLOCK,
                      (SWA_V_SLOT + h) * HEAD_DIM:(SWA_V_SLOT + h + 1) * HEAD_DIM]
            vts_ref[h, b, 0:HEAD_DIM, :] = v.T.astype(vts_ref.dtype)
            vts_ref[h, b, HEAD_DIM:VT_ROWS, :] = ones_blk
    cos = cos_ref[...]
    sin = sin_ref[...]
    lane = lax.broadcasted_iota(jnp.int32, (1, HEAD_DIM), 1)
    lower = (lane % (HEAD_DIM // 2)) < (HEAD_DIM // 4)
    slot = 0
    for count, norm, rope, scale in _QKV_SLOTS:
        for _ in range(count):
            t = p_ref[:, slot * HEAD_DIM:(slot + 1) * HEAD_DIM]
            if norm is not None:
                g = qg_ref[...] if norm == "q" else kg_ref[...]
                t = t * lax.rsqrt(jnp.mean(t * t, axis=-1, keepdims=True) + NORM_EPS) * g
            if rope:
                swapped = jnp.where(lower, pltpu.roll(t, HEAD_DIM - HEAD_DIM // 4, 1),
                                    pltpu.roll(t, HEAD_DIM // 4, 1))
                t = t * cos + swapped * sin
            if scale:
                t = t * (HEAD_DIM ** -0.5 * LOG2_E)
            o_ref[:, slot * HEAD_DIM:(slot + 1) * HEAD_DIM] = t.astype(o_ref.dtype)
            slot += 1


def _qkv_prep(p2, cos, sin, qg, kg):
    m = p2.shape[0]
    tm = KV_CHUNK
    return pl.pallas_call(
        _qkv_kernel,
        grid=(m // tm,),
        in_specs=[
            pl.BlockSpec((tm, N_QKV), lambda i: (i, 0)),
            pl.BlockSpec((tm, HEAD_DIM), lambda i: (i, 0)),
            pl.BlockSpec((tm, HEAD_DIM), lambda i: (i, 0)),
            pl.BlockSpec((1, HEAD_DIM), lambda i: (0, 0)),
            pl.BlockSpec((1, HEAD_DIM), lambda i: (0, 0)),
        ],
        out_specs=[
            pl.BlockSpec((tm, N_QKV), lambda i: (i, 0)),
            pl.BlockSpec((GLB_KV_HEADS, 1, VT_ROWS, tm), lambda i: (0, i, 0, 0)),
            pl.BlockSpec((SWA_KV_HEADS, tm // ATT_BLOCK, VT_ROWS, ATT_BLOCK), lambda i: (0, i, 0, 0)),
        ],
        out_shape=[
            jax.ShapeDtypeStruct((m, N_QKV), MXU_DTYPE),
            jax.ShapeDtypeStruct((GLB_KV_HEADS, m // tm, VT_ROWS, tm), MXU_DTYPE),
            jax.ShapeDtypeStruct((SWA_KV_HEADS, m // ATT_BLOCK, VT_ROWS, ATT_BLOCK), MXU_DTYPE),
        ],
        compiler_params=_params("parallel"),
        name="qkv_prep",
    )(p2, cos, sin, qg.reshape(1, HEAD_DIM), kg.reshape(1, HEAD_DIM))


def _rope_tables(s, nctx):
    t = jnp.arange(s, dtype=jnp.int32)
    pos_r = (t // GRID_W).astype(F32)
    pos_c = (t % GRID_W).astype(F32)
    n_freq = HEAD_DIM // 4
    inv = ROPE_THETA ** (-jnp.arange(n_freq, dtype=F32) / n_freq)
    ar, ac = pos_r[:, None] * inv, pos_c[:, None] * inv
    cos = jnp.concatenate([jnp.cos(ar), jnp.cos(ar), jnp.cos(ac), jnp.cos(ac)], axis=1)
    sin = jnp.concatenate([-jnp.sin(ar), jnp.sin(ar), -jnp.sin(ac), jnp.sin(ac)], axis=1)
    cos = jnp.concatenate([jnp.ones((nctx, HEAD_DIM), F32), cos], axis=0)
    sin = jnp.concatenate([jnp.zeros((nctx, HEAD_DIM), F32), sin], axis=0)
    return cos, sin


def _stack_heads(q, n):
    return jnp.concatenate([q[:, g * HEAD_DIM:(g + 1) * HEAD_DIM] for g in range(n)], axis=0)


def _unstack_heads(o, n, rows):
    return jnp.concatenate([o[g * rows:(g + 1) * rows, :] for g in range(n)], axis=1)


_NT = (((1,), (1,)), ((), ()))


def _glb_kernel(q_ref, k_ref, vt_ref, o_ref, acc_ref, *, tq, nk, unroll, gq):
    tk = KV_CHUNK
    acc_ref[...] = jnp.zeros_like(acc_ref)
    qs = [q_ref[:, g * HEAD_DIM:(g + 1) * HEAD_DIM] for g in range(gq)]

    def body(jj, ms):
        ms = list(ms)

        def scores(c):
            start = pl.multiple_of((jj * unroll + c) * tk, tk)
            kb = k_ref[pl.ds(start, tk), :]
            return [lax.dot_general(kb, q, _NT, preferred_element_type=F32) for q in qs]

        def softmax(s_ts):
            alphas, p_ts = [], []
            for g in range(gq):
                m_new = jnp.maximum(ms[g], jnp.max(s_ts[g], axis=0, keepdims=True))
                alphas.append(jnp.exp2(ms[g] - m_new))
                p_ts.append(jnp.exp2(s_ts[g] - m_new).astype(MXU_DTYPE))
                ms[g] = m_new
            return alphas, p_ts

        def accumulate(c, alphas, p_ts):
            vtb = vt_ref[jj * unroll + c]
            pvs = [jnp.dot(vtb, p_t, preferred_element_type=F32) for p_t in p_ts]
            for g in range(gq):
                acc_ref[g] = alphas[g] * acc_ref[g] + pvs[g]

        s_cur = scores(0)
        s_next = scores(1) if unroll > 1 else None
        prob = softmax(s_cur)
        for c in range(1, unroll):
            s_cur = s_next
            s_next = scores(c + 1) if c + 1 < unroll else None
            accumulate(c - 1, *prob)
            prob = softmax(s_cur)
        accumulate(unroll - 1, *prob)
        return tuple(ms)

    init = tuple(jnp.full((1, tq), NEG_BIG, F32) for _ in range(gq))
    lax.fori_loop(0, nk // unroll, body, init)
    for g in range(gq):
        acc = acc_ref[g]
        out_t = acc[0:HEAD_DIM, :] / acc[HEAD_DIM:HEAD_DIM + 1, :]
        o_ref[:, g * HEAD_DIM:(g + 1) * HEAD_DIM] = out_t.T.astype(o_ref.dtype)


def _glb_attention(qkv, vt, nctx):
    m = qkv.shape[0]
    s = m - nctx
    gq = 8 // GLB_KV_HEADS
    tq = KV_CHUNK
    qw = gq * HEAD_DIM
    q_blk0 = (12 * HEAD_DIM) // qw
    nk = m // KV_CHUNK
    unroll = _pick(nk, (13, 5, 3, 2, 1))
    return pl.pallas_call(
        functools.partial(_glb_kernel, tq=tq, nk=nk, unroll=unroll, gq=gq),
        grid=(GLB_KV_HEADS, s // tq),
        in_specs=[
            pl.BlockSpec((tq, qw), lambda h, i: (i + nctx // tq, q_blk0 + h)),
            pl.BlockSpec((m, HEAD_DIM), lambda h, i: (0, 20 + h)),
            pl.BlockSpec((None, nk, VT_ROWS, KV_CHUNK), lambda h, i: (h, 0, 0, 0)),
        ],
        out_specs=pl.BlockSpec((tq, qw), lambda h, i: (i + nctx // tq, h)),
        out_shape=jax.ShapeDtypeStruct((m, GLB_KV_HEADS * qw), MXU_DTYPE),
        scratch_shapes=[pltpu.VMEM((gq, VT_ROWS, tq), F32)],
        compiler_params=_params("parallel", "parallel"),
        name="global_attention",
    )(qkv, qkv, vt)


def _swa_kernel(q_ref, k_ref, vt_ref, sink_ref, o_ref, *, nctx, nb, gq):
    blk = ATT_BLOCK
    n = pl.program_id(1)
    base = nctx + n * blk
    starts = (pl.multiple_of(base - blk, blk), pl.multiple_of(base, blk),
              pl.multiple_of(jnp.minimum(base + blk, nctx + (nb - 1) * blk), blk))
    kj = lax.broadcasted_iota(jnp.int32, (blk, 2 * blk), 0)
    qi = lax.broadcasted_iota(jnp.int32, (blk, 2 * blk), 1) % blk
    k_parts = [k_ref[pl.ds(st, blk), :] for st in starts] + [k_ref[0:nctx, :]]
    vt = jnp.concatenate([vt_ref[st // blk] for st in starts]
                         + [vt_ref[c] for c in range(nctx // blk)], axis=1)
    pairs = range(gq // 2)
    q2s = [jnp.concatenate([q_ref[:, (2 * pr + r) * HEAD_DIM:(2 * pr + r + 1) * HEAD_DIM]
                            for r in range(2)], axis=0) for pr in pairs]
    scores = [[lax.dot_general(kp, q2, _NT, preferred_element_type=F32) for kp in k_parts]
              for q2 in q2s]
    probs, maxes = [], []
    for pr in pairs:
        s = scores[pr]
        s[0] = jnp.where(jnp.logical_and(kj >= qi, n > 0), s[0], NEG_BIG)
        s[2] = jnp.where(jnp.logical_and(kj <= qi, n < nb - 1), s[2], NEG_BIG)
        m_i = sink_ref[pr]
        for t in s:
            m_i = jnp.maximum(m_i, jnp.max(t, axis=0, keepdims=True))
        probs.append(jnp.concatenate([jnp.exp2(t - m_i).astype(MXU_DTYPE) for t in s], axis=0))
        maxes.append(m_i)
    accs = [jnp.dot(vt, p, preferred_element_type=F32) for p in probs]
    for pr in pairs:
        acc = accs[pr]
        l_i = acc[HEAD_DIM:HEAD_DIM + 1, :] + jnp.exp2(sink_ref[pr] - maxes[pr])
        out = (acc[0:HEAD_DIM, :] / l_i).T
        for r in range(2):
            lo = (2 * pr + r) * HEAD_DIM
            o_ref[:, lo:lo + HEAD_DIM] = out[r * blk:(r + 1) * blk, :].astype(o_ref.dtype)


def _sink_rows(sink, kv_heads, rows_per_head):
    gq = sink.shape[0] // kv_heads
    t = jnp.repeat(sink.reshape(kv_heads, gq).astype(F32) * LOG2_E, rows_per_head, axis=1)
    return jnp.broadcast_to(t[:, :, None], (kv_heads, gq * rows_per_head, 128))


def _swa_attention(qkv, vt, sink, nctx):
    m = qkv.shape[0]
    s = m - nctx
    blk = ATT_BLOCK
    gq = 8 // SWA_KV_HEADS
    nb = s // blk
    qw = gq * HEAD_DIM
    sink_rows = jnp.repeat(sink.astype(F32) * LOG2_E, blk).reshape(SWA_KV_HEADS, gq // 2, 1, 2 * blk)
    return pl.pallas_call(
        functools.partial(_swa_kernel, nctx=nctx, nb=nb, gq=gq),
        grid=(SWA_KV_HEADS, nb),
        in_specs=[
            pl.BlockSpec((blk, qw), lambda h, i: (i + nctx // blk, h)),
            pl.BlockSpec((m, HEAD_DIM), lambda h, i: (0, 8 + h)),
            pl.BlockSpec((None, m // blk, VT_ROWS, blk), lambda h, i: (h, 0, 0, 0)),
            pl.BlockSpec((None, gq // 2, 1, 2 * blk), lambda h, i: (h, 0, 0, 0)),
        ],
        out_specs=pl.BlockSpec((blk, qw), lambda h, i: (i + nctx // blk, h)),
        out_shape=jax.ShapeDtypeStruct((m, SWA_KV_HEADS * qw), MXU_DTYPE),
        compiler_params=_params("parallel", "parallel"),
        name="window_attention",
    )(qkv, qkv, vt, sink_rows)


def _ctx_kernel(*refs, gq, nctx, has_sink):
    if has_sink:
        q_ref, k_ref, v_ref, sink_ref, _, o_ref = refs
    else:
        q_ref, k_ref, v_ref, _, o_ref = refs
    qs = _stack_heads(q_ref[...], gq)
    s = lax.dot_general(qs, k_ref[...], _NT, preferred_element_type=F32)
    m_i = jnp.max(s, axis=-1, keepdims=True)
    if has_sink:
        sink = sink_ref[:, 0:1]
        m_i = jnp.maximum(m_i, sink)
    p = jnp.exp2(s - m_i)
    l_i = jnp.sum(p, axis=-1, keepdims=True)
    if has_sink:
        l_i = l_i + jnp.exp2(sink - m_i)
    acc = jnp.dot(p.astype(MXU_DTYPE), v_ref[...], preferred_element_type=F32)
    o_ref[...] = _unstack_heads(acc / l_i, gq, nctx).astype(o_ref.dtype)


def _ctx_attention(qkv, dst, nctx, q_slot, k_slot, v_slot, sink):
    gq = 4
    qw = gq * HEAD_DIM
    has_sink = sink is not None
    in_specs = [
        pl.BlockSpec((nctx, qw), lambda h: (0, q_slot // gq + h)),
        pl.BlockSpec((nctx, HEAD_DIM), lambda h: (0, k_slot + h)),
        pl.BlockSpec((nctx, HEAD_DIM), lambda h: (0, v_slot + h)),
    ]
    args = [qkv, qkv, qkv]
    if has_sink:
        in_specs.append(pl.BlockSpec((None, gq * nctx, 128), lambda h: (h, 0, 0)))
        args.append(_sink_rows(sink, 2, nctx))
    in_specs.append(pl.BlockSpec(memory_space=pl.ANY))
    args.append(dst)
    return pl.pallas_call(
        functools.partial(_ctx_kernel, gq=gq, nctx=nctx, has_sink=has_sink),
        grid=(2,),
        in_specs=in_specs,
        out_specs=pl.BlockSpec((nctx, qw), lambda h: (0, h)),
        out_shape=jax.ShapeDtypeStruct(dst.shape, dst.dtype),
        input_output_aliases={len(args) - 1: 0},
        compiler_params=_params("parallel"),
        name="context_attention",
    )(*args)


def _top_rows(s, count):
    rows = []
    for _ in range(count):
        mx = jnp.max(s, axis=0, keepdims=True)
        rows.append(mx)
        s = jnp.where(s == mx, NEG_BIG, s)
    return rows


def _route_kernel(q_ref, keys_ref, o_ref, *, n_keys, heads):
    for hh in range(heads):
        _route_one_head(q_ref, keys_ref, o_ref, hh, n_keys)


def _route_one_head(q_ref, keys_ref, o_ref, hh, n_keys):
    tops = []
    for half in range(2):
        lo_lane = (2 * hh + half) * PEER_HALF
        qh = q_ref[:, lo_lane:lo_lane + PEER_HALF]
        s_t = lax.dot_general(keys_ref[hh, half], qh, _NT, preferred_element_type=F32)
        tops.append(_top_rows(s_t[0:n_keys, :], PEER_TOPK + 1))
    a_top, b_top = tops
    a_all = jnp.concatenate(a_top[:PEER_TOPK], axis=0)
    b_all = jnp.concatenate(b_top[:PEER_TOPK], axis=0)
    lo, hi = slice(0, 8), slice(8, PEER_TOPK)
    upper = lax.broadcasted_iota(jnp.int32, (8, 1), 0) >= 4
    cand = jnp.concatenate(
        [a_top[0] + b_all[lo], a_top[0] + b_all[hi], b_top[0] + a_all[hi]]
        + [a_top[i] + b_all[lo] for i in (1, 2, 3)]
        + [jnp.where(upper, b_top[j] + a_all[lo], NEG_BIG) for j in (0, 1, 2)], axis=0)
    c = _top_rows(cand, PEER_TOPK + 1)
    runner_up = jnp.maximum(c[PEER_TOPK], jnp.maximum(a_top[PEER_TOPK] + b_top[0],
                                                       a_top[0] + b_top[PEER_TOPK]))
    thr = 0.5 * (c[PEER_TOPK - 1] + runner_up)
    z = jnp.zeros_like(thr)
    for kk in range(PEER_TOPK):
        z = z + jnp.exp(c[kk] - c[0])
    pad = jnp.zeros((6, thr.shape[1]), F32)
    o_ref[hh] = jnp.concatenate([thr, c[0] + jnp.log(z), pad], axis=0)


def _peer_route(q, keys_pad, n_keys):
    m = q.shape[0]
    tm = _pick(m, (256, 128))
    heads = ROUTE_HEADS_PER_STEP
    return pl.pallas_call(
        functools.partial(_route_kernel, n_keys=n_keys, heads=heads),
        grid=(m // tm, PEER_HEADS // heads),
        in_specs=[
            pl.BlockSpec((tm, heads * 2 * PEER_HALF), lambda i, h: (i, h)),
            pl.BlockSpec((heads, 2, 128, PEER_HALF), lambda i, h: (h, 0, 0, 0)),
        ],
        out_specs=pl.BlockSpec((heads, 8, tm), lambda i, h: (h, 0, i)),
        out_shape=jax.ShapeDtypeStruct((PEER_HEADS, 8, m), F32),
        compiler_params=_params("parallel", "parallel"),
        name="peer_route",
    )(q, keys_pad)


def _peer_w_kernel(h_ref, u_ref, q_ref, kx_ref, r_ref, o_ref, *, slab):
    kd = 2 * PEER_HALF
    for c in range(o_ref.shape[1] // slab):
        sl = slice(c * slab, (c + 1) * slab)
        a = lax.dot_general(h_ref[...], u_ref[sl, :], _NT, preferred_element_type=F32)
        act = 0.5 * a * (1.0 + lax.erf(a * (2.0 ** -0.5)))
        g = None
        for h in range(PEER_HEADS):
            s = jnp.dot(q_ref[:, h * kd:(h + 1) * kd], kx_ref[h, :, sl],
                        preferred_element_type=F32)
            thr = r_ref[:, h:h + 1]
            log_norm = r_ref[:, PEER_HEADS + h:PEER_HEADS + h + 1]
            t = jnp.where(s >= thr, jnp.exp(s - log_norm), 0.0)
            g = t if g is None else g + t
        o_ref[:, sl] = (g * act).astype(o_ref.dtype)


def _peer_weights(h2, u_all, layer, q, kexp, route):
    m, d = h2.shape
    n_exp = u_all.shape[1]
    tm = _pick(m, (256, 128))
    slab = _pick(n_exp, (256, 128))
    te = _pick(n_exp, (5 * slab, slab))
    return pl.pallas_call(
        functools.partial(_peer_w_kernel, slab=slab),
        grid=(n_exp // te, m // tm),
        in_specs=[
            pl.BlockSpec((tm, d), lambda e, i: (i, 0)),
            pl.BlockSpec((None, te, d), lambda e, i: (layer, e, 0)),
            pl.BlockSpec((tm, q.shape[1]), lambda e, i: (i, 0)),
            pl.BlockSpec((PEER_HEADS, 2 * PEER_HALF, te), lambda e, i: (0, 0, e)),
            pl.BlockSpec((tm, route.shape[1]), lambda e, i: (i, 0)),
        ],
        out_specs=pl.BlockSpec((tm, te), lambda e, i: (i, e)),
        out_shape=jax.ShapeDtypeStruct((m, n_exp), MXU_DTYPE),
        compiler_params=_params("parallel", "parallel"),
        name="peer_expert_weights",
    )(h2, u_all, q, kexp, route)


def _peer_key_tables(keys):
    nh, _, nk, kd = keys.shape
    keys_pad = jnp.zeros((nh, 2, 128, kd), keys.dtype).at[:, :, :nk].set(keys)
    k1 = jnp.repeat(keys[:, 0], nk, axis=1)
    k2 = jnp.tile(keys[:, 1], (1, nk, 1))
    kexp = jnp.swapaxes(jnp.concatenate([k1, k2], axis=-1), 1, 2)
    return keys_pad.astype(MXU_DTYPE), kexp.astype(MXU_DTYPE)


def kernel(x, c, ctx, c_ctx, ada_w, ada_b, norm1_g, w_in, conv_w, conv_b, dt_bias, a_log, d_skip,
           ssd_norm_g, swa_sink, q_norm_g, k_norm_g, w_out, norm2_g, peer_wq, peer_keys, peer_u,
           peer_v, final_g):
    assert x.shape[0] == 1 and c.shape[0] == 1 and ctx.shape[0] == 1
    s, d = x.shape[1], x.shape[2]
    nctx = ctx.shape[1]
    depth = ada_w.shape[0]
    n_heads = d_skip.shape[1]
    n_xs = n_heads * SSD_HEAD_DIM
    n_conv = conv_w.shape[2]
    n_keys = peer_keys.shape[3]
    assert n_xs == SSD_GROUPS * 4 * SSD_HEAD_DIM and n_conv == n_xs + 2 * SSD_GROUPS * SSD_STATE
    assert nctx % 256 == 0 and s % 256 == 0 and peer_keys.shape[4] == PEER_HALF
    bf = MXU_DTYPE

    cvec = jnp.zeros((8, d), F32).at[0].set(c_ctx).at[1].set(c[0])
    mods = _mods(cvec, ada_w, ada_b)[:, 0:2].reshape(depth, 2, 6, d)
    cos, sin = _rope_tables(s, nctx)
    xs = jnp.concatenate([ctx[0], x[0]], axis=0)

    o_z, o_xbc, o_dt = 0, n_xs, n_xs + n_conv
    o_att = o_dt + 2 * n_heads
    pad_dt = jnp.zeros((d, 128 - 2 * n_heads), F32)
    w_out_b, u_b, v_b = (t.astype(bf) for t in (w_out, peer_u, peer_v))

    for l in range(depth):
        ml = [mods[l, :, i] for i in range(6)]
        w = w_in[l]
        w_a = jnp.concatenate([w[:, o_xbc:o_dt], w[:, o_z:o_xbc]], axis=1).astype(bf)
        w_b = jnp.concatenate([w[:, o_att:], w[:, o_dt:o_att], pad_dt], axis=1).astype(bf)

        h1 = _normmod(xs, norm1_g[l], ml[0], ml[1], nctx)
        zx = _matmul(h1, w_a, F32, name="in_proj_ssd")
        p2 = _matmul(h1, w_b, F32, name="in_proj_attn")

        conv_w8 = jnp.zeros((8, n_conv), F32).at[:CONV_K].set(conv_w[l])
        zpad = jnp.zeros((128 - 2 * n_heads,), F32)
        dtb = jnp.concatenate([dt_bias[l].reshape(-1), zpad]).reshape(1, 128)
        a_vec = jnp.concatenate([-jnp.exp(a_log[l].reshape(-1)), zpad]).reshape(1, 128)
        dsk = jnp.repeat(d_skip[l], SSD_HEAD_DIM).reshape(1, n_xs)
        ng = ssd_norm_g[l].reshape(1, n_xs)
        cb = conv_b[l].reshape(1, n_conv)
        yf, u = _ssd_forward(zx, p2, conv_w8, cb, dtb, a_vec, nctx)
        ssd_out = _ssd(u, zx, p2, dtb, a_vec, dsk, ng, nctx, reverse=True, yf=yf)

        qkv, vt, vt_swa = _qkv_prep(p2, cos, sin, q_norm_g[l], k_norm_g[l])
        swa_out = _swa_attention(qkv, vt_swa, swa_sink[l], nctx)
        swa_out = _ctx_attention(qkv, swa_out, nctx, 0, 8, 10, swa_sink[l])
        glb_out = _glb_attention(qkv, vt, nctx)
        glb_out = _ctx_attention(qkv, glb_out, nctx, 12, 20, GLB_V_SLOT, None)
        xs = _matmul_parts([ssd_out, swa_out, glb_out], w_out_b, l,
                           res=xs, gate=ml[2], nctx=nctx, name="out_proj")

        h2 = _normmod(xs, norm2_g[l], ml[3], ml[4], nctx)
        q = _matmul_f32w(h2, peer_wq, l, bf, n_out=peer_wq.shape[2], name="peer_query",
                         col_block=lambda j, tn: j)
        keys_pad, kexp = _peer_key_tables(peer_keys[l])
        route = _peer_route(q, keys_pad, n_keys)
        route = jnp.transpose(route[:, 0:2], (2, 1, 0)).reshape(nctx + s, 2 * PEER_HEADS)
        wts = _peer_weights(h2, u_b, l, q, kexp, route)
        xs = _matmul(wts, v_b, F32, layer=l, res=xs, gate=ml[5], nctx=nctx, name="peer_out")

    return _finalnorm(xs, final_g, nctx)[None]
```
